```python
import jax, jax.numpy as jnp
from jax import lax
import numpy as np

D_MODEL = 1024
BATCH = 2
SEQ = 16384
DEPTH = 1

MLA_HEADS = 8
QK_NOPE_DIM = 64
QK_ROPE_DIM = 32
QK_DIM = QK_NOPE_DIM + QK_ROPE_DIM
V_HEAD_DIM = 64
MLA_WIDTH = MLA_HEADS * V_HEAD_DIM
Q_LORA_RANK = 256
KV_LORA_RANK = 128
ROPE_THETA = 10000.0
Q_BLOCK = 128
SGU_GROUPS = 8
SGU_GROUP_DIM = 64
SGU_WIDTH = SGU_GROUPS * SGU_GROUP_DIM
SGU_CHUNK = 128
N_GROUPS = 4
EXPERTS_PER_GROUP = 8
N_EXPERTS = N_GROUPS * EXPERTS_PER_GROUP
TOP_K = 2
EXPERT_FF = 512
MOE_BLOCK = 256
N_BRANCHES = 2
N_MOD = 6
EPS = 1e-6
IN_WIDTH = Q_LORA_RANK + KV_LORA_RANK + QK_ROPE_DIM + 2 * SGU_WIDTH + N_BRANCHES * D_MODEL

kernel_name = "hybrid_mla_sgu_hmoe_adaln_block"


def rmsnorm(x, g):
    xf = x.astype(jnp.float32)
    y = xf * lax.rsqrt(jnp.mean(xf * xf, axis=-1, keepdims=True) + EPS)
    return (y * g.astype(jnp.float32)).astype(x.dtype)


def layernorm(x, g, b):
    xf = x.astype(jnp.float32)
    mu = jnp.mean(xf, axis=-1, keepdims=True)
    var = jnp.mean(jnp.square(xf - mu), axis=-1, keepdims=True)
    y = (xf - mu) * lax.rsqrt(var + EPS)
    return (y * g.astype(jnp.float32) + b.astype(jnp.float32)).astype(x.dtype)


def rope(x, positions):
    half = x.shape[-1] // 2
    freqs = ROPE_THETA ** (-jnp.arange(half, dtype=jnp.float32) / half)
    ang = positions.astype(jnp.float32)[..., None] * freqs
    cos = jnp.cos(ang)[:, :, None, :]
    sin = jnp.sin(ang)[:, :, None, :]
    xf = x.astype(jnp.float32)
    x1, x2 = xf[..., :half], xf[..., half:]
    out = jnp.concatenate([x1 * cos - x2 * sin, x1 * sin + x2 * cos], axis=-1)
    return out.astype(x.dtype)


def mla(c_q, c_kv, k_rope, positions, q_norm_g, w_uq, kv_norm_g, w_ukv):
    B, S, _ = c_q.shape
    q = (rmsnorm(c_q, q_norm_g) @ w_uq).reshape(B, S, MLA_HEADS, QK_DIM)
    q = jnp.concatenate([q[..., :QK_NOPE_DIM], rope(q[..., QK_NOPE_DIM:], positions)], axis=-1)
    kv = (rmsnorm(c_kv, kv_norm_g) @ w_ukv).reshape(B, S, MLA_HEADS, QK_NOPE_DIM + V_HEAD_DIM)
    k_nope, v = kv[..., :QK_NOPE_DIM], kv[..., QK_NOPE_DIM:]
    k_pe = rope(k_rope[:, :, None, :], positions)
    k = jnp.concatenate([k_nope, jnp.broadcast_to(k_pe, (B, S, MLA_HEADS, QK_ROPE_DIM))], axis=-1)
    scale = QK_DIM ** -0.5
    nb = S // Q_BLOCK
    qb = q.reshape(B, nb, Q_BLOCK, MLA_HEADS, QK_DIM).transpose(1, 0, 2, 3, 4)
    k_idx = jnp.arange(S)

    def attend_block(args):
        q_blk, i = args
        s = jnp.einsum('bqhd,bkhd->bhqk', q_blk, k).astype(jnp.float32) * scale
        q_idx = i * Q_BLOCK + jnp.arange(Q_BLOCK)
        causal = k_idx[None, :] <= q_idx[:, None]
        s = jnp.where(causal[None, None], s, -jnp.inf)
        p = jax.nn.softmax(s, axis=-1).astype(v.dtype)
        return jnp.einsum('bhqk,bkhd->bqhd', p, v)

    o = lax.map(attend_block, (qb, jnp.arange(nb)))
    return o.transpose(1, 0, 2, 3, 4).reshape(B, S, MLA_WIDTH)


def spatial_gating(u, v, v_norm_g, v_norm_b, w_s, b_s):
    u = jax.nn.gelu(u)
    v = layernorm(jax.nn.gelu(v), v_norm_g, v_norm_b)
    B, S, _ = v.shape
    nc = S // SGU_CHUNK
    vc = v.reshape(B, nc, SGU_CHUNK, SGU_GROUPS, SGU_GROUP_DIM)
    tril = jnp.tril(jnp.ones((SGU_CHUNK, SGU_CHUNK), dtype=bool))
    ws = jnp.where(tril[None], w_s, jnp.zeros_like(w_s))
    s = jnp.einsum('gts,bnsgc->bntgc', ws, vc) + b_s.T[None, None, :, :, None]
    return u * s.reshape(B, S, SGU_WIDTH)


def hierarchical_moe(h, w_rg, b_rg, w_re, b_re, w1, w3, w2):
    B, S, D = h.shape
    xf = h.reshape(-1, D)
    N = xf.shape[0]
    g_logits = (xf @ w_rg).astype(jnp.float32) + b_rg.astype(jnp.float32)
    g_prob = jax.nn.softmax(g_logits, axis=-1)
    grp = jnp.argmax(g_logits, axis=-1)
    p_grp = jnp.take_along_axis(g_prob, grp[:, None], axis=-1)
    e_logits = ((xf @ w_re).astype(jnp.float32) + b_re.astype(jnp.float32)).reshape(N, N_GROUPS, EXPERTS_PER_GROUP)
    e_in = jnp.take_along_axis(e_logits, grp[:, None, None], axis=1)[:, 0]
    e_prob = jax.nn.softmax(e_in, axis=-1)
    top_p, top_i = lax.top_k(e_prob, TOP_K)
    top_p = top_p / jnp.sum(top_p, axis=-1, keepdims=True)
    weights = p_grp * top_p
    experts = grp[:, None] * EXPERTS_PER_GROUP + top_i
    A = N * TOP_K
    flat_e = experts.reshape(-1)
    flat_tok = jnp.arange(A, dtype=jnp.int32) // TOP_K
    flat_w = weights.reshape(-1)
    order = jnp.argsort(flat_e)
    sorted_e = flat_e[order]
    counts = jnp.zeros((N_EXPERTS,), jnp.int32).at[flat_e].add(1)
    starts = jnp.cumsum(counts) - counts
    padded = ((counts + MOE_BLOCK - 1) // MOE_BLOCK) * MOE_BLOCK
    pad_ends = jnp.cumsum(padded)
    pad_starts = pad_ends - padded
    rank = jnp.arange(A, dtype=jnp.int32) - starts[sorted_e]
    dest = pad_starts[sorted_e] + rank
    P = ((A + MOE_BLOCK - 1) // MOE_BLOCK + N_EXPERTS) * MOE_BLOCK
    nblk = P // MOE_BLOCK
    slot_tok = jnp.zeros((P,), jnp.int32).at[dest].set(flat_tok[order])
    slot_w = jnp.zeros((P,), jnp.float32).at[dest].set(flat_w[order])
    blk_e = jnp.minimum(jnp.searchsorted(pad_ends, jnp.arange(nblk) * MOE_BLOCK, side='right'), N_EXPERTS - 1)
    x_blk = xf[slot_tok].reshape(nblk, MOE_BLOCK, D)

    def expert_block(args):
        xb, e = args
        return (jax.nn.silu(xb @ w1[e]) * (xb @ w3[e])) @ w2[e]

    y = lax.map(expert_block, (x_blk, blk_e)).reshape(P, D)
    out = jnp.zeros((N, D), jnp.float32).at[slot_tok].add(y.astype(jnp.float32) * slot_w[:, None])
    return out.astype(h.dtype).reshape(B, S, D)


def setup_inputs(seed: int = 0) -> dict:
    key = jax.random.key(seed)
    ks = jax.random.split(key, 32)
    f32 = jnp.float32

    def nrm(k, shape, fan_in, mult=1.0):
        return jax.random.normal(k, shape, f32) * (mult * fan_in ** -0.5)

    def gain(k, shape):
        return 1.0 + 0.02 * jax.random.normal(k, shape, f32)

    L = DEPTH
    x = jax.random.normal(ks[0], (BATCH, SEQ, D_MODEL), f32)
    c = jax.random.normal(ks[1], (BATCH, D_MODEL), f32)
    offs = jax.random.randint(ks[2], (BATCH, 1), 0, 4096, dtype=jnp.int32)
    positions = jnp.arange(SEQ, dtype=jnp.int32)[None, :] + offs
    return {
        "x": x,
        "c": c,
        "positions": positions,
        "w_ada": nrm(ks[3], (L, D_MODEL, N_MOD * D_MODEL), D_MODEL),
        "b_ada": 0.01 * jax.random.normal(ks[4], (L, N_MOD * D_MODEL), f32),
        "norm1_g": gain(ks[5], (L, D_MODEL)),
        "w_in": nrm(ks[6], (L, D_MODEL, IN_WIDTH), D_MODEL),
        "q_norm_g": gain(ks[7], (L, Q_LORA_RANK)),
        "w_uq": nrm(ks[8], (L, Q_LORA_RANK, MLA_HEADS * QK_DIM), Q_LORA_RANK),
        "kv_norm_g": gain(ks[9], (L, KV_LORA_RANK)),
        "w_ukv": nrm(ks[10], (L, KV_LORA_RANK, MLA_HEADS * (QK_NOPE_DIM + V_HEAD_DIM)), KV_LORA_RANK),
        "v_norm_g": gain(ks[11], (L, SGU_WIDTH)),
        "v_norm_b": 0.01 * jax.random.normal(ks[12], (L, SGU_WIDTH), f32),
        "w_s": nrm(ks[13], (L, SGU_GROUPS, SGU_CHUNK, SGU_CHUNK), SGU_CHUNK),
        "b_s": gain(ks[14], (L, SGU_GROUPS, SGU_CHUNK)),
        "w_br_mla": nrm(ks[15], (L, MLA_WIDTH, D_MODEL), MLA_WIDTH),
        "w_br_sgu": nrm(ks[16], (L, SGU_WIDTH, D_MODEL), SGU_WIDTH),
        "w_out": nrm(ks[17], (L, D_MODEL, D_MODEL), D_MODEL),
        "norm2_g": gain(ks[18], (L, D_MODEL)),
        "w_rg": nrm(ks[19], (L, D_MODEL, N_GROUPS), D_MODEL),
        "b_rg": 0.01 * jax.random.normal(ks[20], (L, N_GROUPS), f32),
        "w_re": nrm(ks[21], (L, D_MODEL, N_EXPERTS), D_MODEL),
        "b_re": 0.01 * jax.random.normal(ks[22], (L, N_EXPERTS), f32),
        "w1": nrm(ks[23], (L, N_EXPERTS, D_MODEL, EXPERT_FF), D_MODEL),
        "w3": nrm(ks[24], (L, N_EXPERTS, D_MODEL, EXPERT_FF), D_MODEL),
        "w2": nrm(ks[25], (L, N_EXPERTS, EXPERT_FF, D_MODEL), EXPERT_FF),
        "final_g": gain(ks[26], (D_MODEL,)),
    }


def reference(x, c, positions, w_ada, b_ada, norm1_g, w_in, q_norm_g, w_uq, kv_norm_g, w_ukv,
              v_norm_g, v_norm_b, w_s, b_s, w_br_mla, w_br_sgu, w_out, norm2_g,
              w_rg, b_rg, w_re, b_re, w1, w3, w2, final_g):
    split_idx = list(np.cumsum([Q_LORA_RANK, KV_LORA_RANK, QK_ROPE_DIM, SGU_WIDTH, SGU_WIDTH, D_MODEL])[:])
    for l in range(DEPTH):
        mod = jax.nn.silu(c) @ w_ada[l] + b_ada[l]
        shift1, scale1, gate1, shift2, scale2, gate2 = [m[:, None, :] for m in jnp.split(mod, N_MOD, axis=-1)]

        h1 = rmsnorm(x, norm1_g[l]) * (1.0 + scale1) + shift1
        proj = h1 @ w_in[l]
        c_q, c_kv, k_rope, sgu_u, sgu_v, gl_mla, gl_sgu = jnp.split(proj, split_idx, axis=-1)
        y_mla = mla(c_q, c_kv, k_rope, positions, q_norm_g[l], w_uq[l], kv_norm_g[l], w_ukv[l]) @ w_br_mla[l]
        y_sgu = spatial_gating(sgu_u, sgu_v, v_norm_g[l], v_norm_b[l], w_s[l], b_s[l]) @ w_br_sgu[l]
        merged = jax.nn.sigmoid(gl_mla) * y_mla + jax.nn.sigmoid(gl_sgu) * y_sgu
        x = x + gate1 * (merged @ w_out[l])

        h2 = rmsnorm(x, norm2_g[l]) * (1.0 + scale2) + shift2
        x = x + gate2 * hierarchical_moe(h2, w_rg[l], b_rg[l], w_re[l], b_re[l], w1[l], w3[l], w2[l])
    return rmsnorm(x, final_g)
```

```python
import functools

import jax
import jax.numpy as jnp
from jax import lax
from jax.experimental import pallas as pl
from jax.experimental.pallas import tpu as pltpu

F32 = jnp.float32
BF16 = jnp.bfloat16

D_MODEL = 1024
MLA_HEADS = 8
QK_NOPE_DIM = 64
QK_ROPE_DIM = 32
ROPE_HALF = QK_ROPE_DIM // 2
QK_DIM = QK_NOPE_DIM + QK_ROPE_DIM
V_HEAD_DIM = 64
Q_LORA_RANK = 256
KV_LORA_RANK = 128
ROPE_THETA = 10000.0
SGU_GROUPS = 8
SGU_GROUP_DIM = 64
SGU_WIDTH = SGU_GROUPS * SGU_GROUP_DIM
SGU_CHUNK = 128
N_GROUPS = 4
EXPERTS_PER_GROUP = 8
N_EXPERTS = N_GROUPS * EXPERTS_PER_GROUP
TOP_K = 2
EXPERT_FF = 512
N_MOD = 6
EPS = 1e-6

LANES = 128
HEAD_PAD = LANES
VMEM_LIMIT = 48 * 1024 * 1024

TOKEN_TILE = 512
ATTN_TILE = 512
EXPERT_BLOCK = 256
ADALN_TILE = 512

_SEG_CQ = (0, 256)
_SEG_CKV = (256, 384)
_SEG_KR = (384, 512)
_SEG_U = (512, 1024)
_SEG_V = (1024, 1536)
_SEG_GMLA = (1536, 2560)
_SEG_GSGU = (2560, 3584)
PACKED_IN = 3584

Q_SCALE = (QK_DIM ** -0.5) * 1.4426950408889634


def _sigmoid(x):
    return 1.0 / (1.0 + jnp.exp(-x))


def _gelu_tanh(x):
    c = 0.7978845608028654
    return 0.5 * x * (1.0 + jnp.tanh(c * (x + 0.044715 * (x * x * x))))


def _rms(x, g):
    return x * lax.rsqrt(jnp.mean(x * x, axis=-1, keepdims=True) + EPS) * g


def _dot(a, b):
    return jnp.dot(a, b, preferred_element_type=F32)


def _adaln_kernel(ct_ref, w_ref, b_ref, o_ref):
    ct = ct_ref[...]
    s = ct * _sigmoid(ct)
    w = w_ref[...]
    rows = [jnp.sum(s[:, b:b + 1] * w, axis=0, keepdims=True) for b in range(ct.shape[1])]
    o_ref[...] = jnp.concatenate(rows, axis=0) + b_ref[...]


def _adaln(c, w_ada, b_ada):
    batch = c.shape[0]
    width = w_ada.shape[1]
    return pl.pallas_call(
        _adaln_kernel,
        grid=(width // ADALN_TILE,),
        in_specs=[
            pl.BlockSpec((D_MODEL, batch), lambda j: (0, 0)),
            pl.BlockSpec((D_MODEL, ADALN_TILE), lambda j: (0, j)),
            pl.BlockSpec((1, ADALN_TILE), lambda j: (0, j)),
        ],
        out_specs=pl.BlockSpec((batch, ADALN_TILE), lambda j: (0, j)),
        out_shape=jax.ShapeDtypeStruct((batch, width), F32),
        compiler_params=pltpu.CompilerParams(dimension_semantics=("arbitrary",)),
        name="adaln",
    )(c.T, w_ada, b_ada.reshape(1, width))


def _pre_kernel(x_ref, mod_ref, pos_ref, g1_ref, win_ref, qg_ref, wqa_ref, wqb_ref,
                kvg_ref, wkt_ref, wv_ref, vones_ref, fcol_ref,
                vng_ref, vnb_ref, ws_ref, bs_ref, wbs_ref,
                q_out, kt_out, v_out, gm_out, ms_out):
    tm = x_ref.shape[1]
    x = x_ref[0]
    shift1 = mod_ref[0, 0:1, :]
    scale1 = mod_ref[0, 1:2, :]
    h1 = _rms(x, g1_ref[...]) * (1.0 + scale1) + shift1
    hb = h1.astype(BF16)

    def proj(seg):
        return _dot(hb, win_ref[:, seg[0]:seg[1]])

    pos = pos_ref[0].astype(F32)
    ang = fcol_ref[...] * pos
    cos_t = jnp.cos(ang)
    sin_t = jnp.sin(ang)

    cqn = _rms(proj(_SEG_CQ), qg_ref[...]).astype(BF16)
    qa = _dot(cqn, wqa_ref[...])
    qb = _dot(cqn, wqb_ref[...])
    ones_n = jnp.ones((QK_NOPE_DIM, tm), F32)
    zeros_n = jnp.zeros((QK_NOPE_DIM, tm), F32)
    pad_rows = HEAD_PAD - QK_DIM
    c_tab = (jnp.concatenate([ones_n, cos_t, cos_t, jnp.ones((pad_rows, tm), F32)], axis=0) * Q_SCALE).T
    s_tab = (jnp.concatenate([zeros_n, -sin_t, sin_t, jnp.zeros((pad_rows, tm), F32)], axis=0) * Q_SCALE).T
    for h in range(MLA_HEADS):
        sl = slice(h * HEAD_PAD, (h + 1) * HEAD_PAD)
        q_out[0, h] = (qa[:, sl] * c_tab + qb[:, sl] * s_tab).astype(BF16)

    ckvn = _rms(proj(_SEG_CKV), kvg_ref[...]).astype(BF16)
    kr_t = proj(_SEG_KR).T
    x1 = kr_t[0:ROPE_HALF]
    x2 = kr_t[ROPE_HALF:QK_ROPE_DIM]
    k_pe = jnp.concatenate([x1 * cos_t - x2 * sin_t, x1 * sin_t + x2 * cos_t,
                            jnp.zeros((pad_rows, tm), F32)], axis=0)
    for h in range(MLA_HEADS):
        k_nope = lax.dot_general(wkt_ref[h], ckvn, (((1,), (1,)), ((), ())),
                                 preferred_element_type=F32)
        kt_out[0, h, 0] = jnp.concatenate([k_nope, k_pe], axis=0).astype(BF16)
    v_full = _dot(ckvn, wv_ref[...]) + vones_ref[...]
    for h in range(MLA_HEADS):
        sl = slice(h * HEAD_PAD, (h + 1) * HEAD_PAD)
        v_out[0, h] = v_full[:, sl].astype(BF16)

    gu = _gelu_tanh(proj(_SEG_U))
    gv = _gelu_tanh(proj(_SEG_V))
    mu = jnp.mean(gv, axis=-1, keepdims=True)
    dv = gv - mu
    var = jnp.mean(dv * dv, axis=-1, keepdims=True)
    vln = (dv * lax.rsqrt(var + EPS) * vng_ref[...] + vnb_ref[...]).astype(BF16)
    row = lax.broadcasted_iota(jnp.int32, (SGU_CHUNK, SGU_CHUNK), 0)
    col = lax.broadcasted_iota(jnp.int32, (SGU_CHUNK, SGU_CHUNK), 1)
    causal = col <= row
    w_tril = [jnp.where(causal, ws_ref[g], 0.0).astype(BF16) for g in range(SGU_GROUPS)]
    lane_grp = lax.broadcasted_iota(jnp.int32, (SGU_CHUNK, SGU_WIDTH), 1) // SGU_GROUP_DIM
    chunks = []
    for c in range(tm // SGU_CHUNK):
        vc = vln[c * SGU_CHUNK:(c + 1) * SGU_CHUNK]
        acc = jnp.zeros((SGU_CHUNK, SGU_WIDTH), F32)
        for g in range(SGU_GROUPS):
            acc = jnp.where(lane_grp == g, _dot(w_tril[g], vc), acc)
        chunks.append(acc + bs_ref[...])
    sgu = (gu * jnp.concatenate(chunks, axis=0)).astype(BF16)
    y_sgu = _dot(sgu, wbs_ref[...])
    ms_out[0] = (_sigmoid(proj(_SEG_GSGU)) * y_sgu).astype(BF16)
    gm_out[0] = _sigmoid(proj(_SEG_GMLA)).astype(BF16)


def _pre(x, mod, positions, g1, w_in_p, qg, wqa, wqb, kvg, wkt, wv, vones, fcol,
         vng, vnb, w_s, bs_full, wbs):
    batch, seq, _ = x.shape
    tm = TOKEN_TILE
    assert tm == ATTN_TILE and seq % tm == 0
    nt = seq // tm
    hw = MLA_HEADS * HEAD_PAD

    def const(shape):
        return pl.BlockSpec(shape, lambda b, i: (0,) * len(shape))

    return pl.pallas_call(
        _pre_kernel,
        grid=(batch, nt),
        in_specs=[
            pl.BlockSpec((1, tm, D_MODEL), lambda b, i: (b, i, 0)),
            pl.BlockSpec((1, N_MOD, D_MODEL), lambda b, i: (b, 0, 0)),
            pl.BlockSpec((1, 1, tm), lambda b, i: (b, 0, i)),
            const((1, D_MODEL)),
            const((D_MODEL, PACKED_IN)),
            const((1, Q_LORA_RANK)),
            const((Q_LORA_RANK, hw)),
            const((Q_LORA_RANK, hw)),
            const((1, KV_LORA_RANK)),
            const((MLA_HEADS, QK_NOPE_DIM, KV_LORA_RANK)),
            const((KV_LORA_RANK, hw)),
            const((1, hw)),
            const((ROPE_HALF, 1)),
            const((1, SGU_WIDTH)),
            const((1, SGU_WIDTH)),
            const((SGU_GROUPS, SGU_CHUNK, SGU_CHUNK)),
            const((SGU_CHUNK, SGU_WIDTH)),
            const((SGU_WIDTH, D_MODEL)),
        ],
        out_specs=[
            pl.BlockSpec((1, MLA_HEADS, tm, HEAD_PAD), lambda b, i: (b, 0, i, 0)),
            pl.BlockSpec((1, MLA_HEADS, 1, HEAD_PAD, tm), lambda b, i: (b, 0, i, 0, 0)),
            pl.BlockSpec((1, MLA_HEADS, tm, HEAD_PAD), lambda b, i: (b, 0, i, 0)),
            pl.BlockSpec((1, tm, D_MODEL), lambda b, i: (b, i, 0)),
            pl.BlockSpec((1, tm, D_MODEL), lambda b, i: (b, i, 0)),
        ],
        out_shape=[
            jax.ShapeDtypeStruct((batch, MLA_HEADS, seq, HEAD_PAD), BF16),
            jax.ShapeDtypeStruct((batch, MLA_HEADS, nt, HEAD_PAD, tm), BF16),
            jax.ShapeDtypeStruct((batch, MLA_HEADS, seq, HEAD_PAD), BF16),
            jax.ShapeDtypeStruct((batch, seq, D_MODEL), BF16),
            jax.ShapeDtypeStruct((batch, seq, D_MODEL), BF16),
        ],
        compiler_params=pltpu.CompilerParams(
            dimension_semantics=("arbitrary", "arbitrary"), vmem_limit_bytes=VMEM_LIMIT),
        name="pre",
    )(x, mod, positions.reshape(batch, 1, seq), g1, w_in_p, qg, wqa, wqb, kvg, wkt, wv, vones,
      fcol, vng, vnb, w_s, bs_full, wbs)


def _flash_kernel(q_ref, kt_ref, v_ref, o_ref, m_ref, acc_ref):
    t = q_ref.shape[2]
    qi = pl.program_id(2)
    q = q_ref[0, 0]
    m_ref[...] = jnp.full(m_ref.shape, -1e30, F32)
    acc_ref[...] = jnp.zeros(acc_ref.shape, F32)

    def step(j, masked):
        s = _dot(q, kt_ref[0, 0, j])
        if masked:
            row = lax.broadcasted_iota(jnp.int32, (t, t), 0)
            col = lax.broadcasted_iota(jnp.int32, (t, t), 1)
            s = jnp.where(col <= row, s, -jnp.inf)
        m_old = m_ref[...]
        m_new = jnp.maximum(m_old, jnp.max(s, axis=-1, keepdims=True))
        p = jnp.exp2(s - m_new).astype(BF16)
        start = pl.multiple_of(j * t, t)
        pv = _dot(p, v_ref[0, 0, pl.ds(start, t), :])
        acc_ref[...] = jnp.exp2(m_old - m_new) * acc_ref[...] + pv
        m_ref[...] = m_new

    def body(j, carry):
        step(j, False)
        return carry

    lax.fori_loop(0, qi, body, 0)
    step(qi, True)
    acc = acc_ref[...]
    o_ref[0, 0] = (acc / acc[:, V_HEAD_DIM:V_HEAD_DIM + 1]).astype(BF16)


def _flash(q, kt, v):
    batch, heads, seq, _ = q.shape
    t = ATTN_TILE
    nt = seq // t
    return pl.pallas_call(
        _flash_kernel,
        grid=(batch, heads, nt),
        in_specs=[
            pl.BlockSpec((1, 1, t, HEAD_PAD), lambda b, h, i: (b, h, i, 0)),
            pl.BlockSpec((1, 1, nt, HEAD_PAD, t), lambda b, h, i: (b, h, 0, 0, 0)),
            pl.BlockSpec((1, 1, seq, HEAD_PAD), lambda b, h, i: (b, h, 0, 0)),
        ],
        out_specs=pl.BlockSpec((1, 1, t, HEAD_PAD), lambda b, h, i: (b, h, i, 0)),
        out_shape=jax.ShapeDtypeStruct((batch, heads, seq, HEAD_PAD), BF16),
        scratch_shapes=[pltpu.VMEM((t, 1), F32), pltpu.VMEM((t, HEAD_PAD), F32)],
        compiler_params=pltpu.CompilerParams(
            dimension_semantics=("arbitrary", "arbitrary", "arbitrary"),
            vmem_limit_bytes=VMEM_LIMIT),
        name="flash",
    )(q, kt, v)


def _post_kernel(x_ref, o_ref, gm_ref, ms_ref, mod_ref, wbm_ref, wout_ref, g2_ref, rw_ref, rb_ref,
                 x2_out, h2_out, ri_out, rwt_out, cnt_out, carry_ref):
    tm = x_ref.shape[1]

    @pl.when((pl.program_id(0) == 0) & (pl.program_id(1) == 0))
    def _():
        carry_ref[...] = jnp.zeros(carry_ref.shape, F32)

    gate1 = mod_ref[0, 2:3, :]
    shift2 = mod_ref[0, 3:4, :]
    scale2 = mod_ref[0, 4:5, :]
    y_mla = _dot(o_ref[0, 0], wbm_ref[0])
    for h in range(1, MLA_HEADS):
        y_mla = y_mla + _dot(o_ref[0, h], wbm_ref[h])
    merged = gm_ref[0].astype(F32) * y_mla + ms_ref[0].astype(F32)
    x2 = x_ref[0] + gate1 * _dot(merged.astype(BF16), wout_ref[...])
    x2_out[0] = x2
    h2 = _rms(x2, g2_ref[...]) * (1.0 + scale2) + shift2
    h2_out[0] = h2.astype(BF16)

    hi = h2.astype(BF16)
    lo = (h2 - hi.astype(F32)).astype(BF16)
    a = _dot(hi, rw_ref[...])
    b = _dot(lo, rw_ref[:, 0:LANES])
    logits = a[:, 0:LANES] + a[:, LANES:2 * LANES] + b + rb_ref[...]

    lane = lax.broadcasted_iota(jnp.int32, (tm, LANES), 1).astype(F32)
    big = float(LANES)
    ninf = -jnp.inf
    gl = jnp.where(lane < N_GROUPS, logits, ninf)
    gmax = jnp.max(gl, axis=-1, keepdims=True)
    grp = jnp.min(jnp.where(gl == gmax, lane, big), axis=-1, keepdims=True)
    p_grp = 1.0 / jnp.sum(jnp.exp(gl - gmax), axis=-1, keepdims=True)
    first = N_GROUPS + EXPERTS_PER_GROUP * grp
    el = jnp.where((lane >= first) & (lane < first + EXPERTS_PER_GROUP), logits, ninf)
    m1 = jnp.max(el, axis=-1, keepdims=True)
    i1 = jnp.min(jnp.where(el == m1, lane, big), axis=-1, keepdims=True)
    el2 = jnp.where(lane == i1, ninf, el)
    m2 = jnp.max(el2, axis=-1, keepdims=True)
    i2 = jnp.min(jnp.where(el2 == m2, lane, big), axis=-1, keepdims=True)
    tt = jnp.exp(m2 - m1)
    w0 = p_grp / (1.0 + tt)
    w1 = p_grp * tt / (1.0 + tt)
    e0 = i1 - N_GROUPS
    e1 = i2 - N_GROUPS

    onehot = jnp.where((lane == e0) | (lane == e1), 1.0, 0.0)
    row = lax.broadcasted_iota(jnp.int32, (tm, tm), 0)
    col = lax.broadcasted_iota(jnp.int32, (tm, tm), 1)
    lower = jnp.where(col < row, 1.0, 0.0).astype(BF16)
    before = _dot(lower, onehot.astype(BF16)) + carry_ref[...]
    r0 = jnp.sum(jnp.where(lane == e0, before, 0.0), axis=-1, keepdims=True)
    r1 = jnp.sum(jnp.where(lane == e1, before, 0.0), axis=-1, keepdims=True)
    carry_ref[...] = carry_ref[...] + jnp.sum(onehot, axis=0, keepdims=True)
    cnt_out[...] = carry_ref[...]

    ri = jnp.where(lane == 0, e0, jnp.where(lane == 1, e1, jnp.where(lane == 2, r0, jnp.where(lane == 3, r1, 0.0))))
    ri_out[0] = ri.astype(jnp.int32)
    rwt_out[0] = jnp.where(lane == 0, w0, jnp.where(lane == 1, w1, 0.0))


def _post(x, o, gm, ms, mod, wbm, wout, g2, rw, rb):
    batch, seq, _ = x.shape
    tm = TOKEN_TILE
    nt = seq // tm

    def const(shape):
        return pl.BlockSpec(shape, lambda b, i: (0,) * len(shape))

    tok = lambda w: pl.BlockSpec((1, tm, w), lambda b, i: (b, i, 0))
    return pl.pallas_call(
        _post_kernel,
        grid=(batch, nt),
        in_specs=[
            tok(D_MODEL),
            pl.BlockSpec((1, MLA_HEADS, tm, HEAD_PAD), lambda b, i: (b, 0, i, 0)),
            tok(D_MODEL),
            tok(D_MODEL),
            pl.BlockSpec((1, N_MOD, D_MODEL), lambda b, i: (b, 0, 0)),
            const((MLA_HEADS, HEAD_PAD, D_MODEL)),
            const((D_MODEL, D_MODEL)),
            const((1, D_MODEL)),
            const((D_MODEL, 2 * LANES)),
            const((1, LANES)),
        ],
        out_specs=[tok(D_MODEL), tok(D_MODEL), tok(LANES), tok(LANES), const((1, LANES))],
        out_shape=[
            jax.ShapeDtypeStruct((batch, seq, D_MODEL), F32),
            jax.ShapeDtypeStruct((batch, seq, D_MODEL), BF16),
            jax.ShapeDtypeStruct((batch, seq, LANES), jnp.int32),
            jax.ShapeDtypeStruct((batch, seq, LANES), F32),
            jax.ShapeDtypeStruct((1, LANES), F32),
        ],
        scratch_shapes=[pltpu.VMEM((1, LANES), F32)],
        compiler_params=pltpu.CompilerParams(
            dimension_semantics=("arbitrary", "arbitrary"), vmem_limit_bytes=VMEM_LIMIT),
        name="post",
    )(x, o, gm, ms, mod, wbm, wout, g2, rw, rb)


def _expert_kernel(be_ref, nr_ref, x_ref, w13_ref, w2_ref, y_ref):
    j = pl.program_id(0)

    @pl.when(j < nr_ref[0])
    def _():
        h = _dot(x_ref[...], w13_ref[0])
        a = h[:, 0:EXPERT_FF]
        act = (a * _sigmoid(a) * h[:, EXPERT_FF:2 * EXPERT_FF]).astype(BF16)
        y_ref[...] = _dot(act, w2_ref[0]).astype(BF16)

    @pl.when(j >= nr_ref[0])
    def _():
        y_ref[...] = jnp.zeros(y_ref.shape, BF16)


def _experts(blk_e, n_real, x_sorted, w13, w2):
    rows = x_sorted.shape[0]
    nblk = rows // EXPERT_BLOCK
    last = lambda j, be, nr: jnp.minimum(j, nr[0] - 1)
    return pl.pallas_call(
        _expert_kernel,
        grid_spec=pltpu.PrefetchScalarGridSpec(
            num_scalar_prefetch=2,
            grid=(nblk,),
            in_specs=[
                pl.BlockSpec((EXPERT_BLOCK, D_MODEL), lambda j, be, nr: (last(j, be, nr), 0)),
                pl.BlockSpec((1, D_MODEL, 2 * EXPERT_FF), lambda j, be, nr: (be[last(j, be, nr)], 0, 0)),
                pl.BlockSpec((1, EXPERT_FF, D_MODEL), lambda j, be, nr: (be[last(j, be, nr)], 0, 0)),
            ],
            out_specs=pl.BlockSpec((EXPERT_BLOCK, D_MODEL), lambda j, be, nr: (j, 0)),
        ),
        out_shape=jax.ShapeDtypeStruct((rows, D_MODEL), BF16),
        compiler_params=pltpu.CompilerParams(
            dimension_semantics=("arbitrary",), vmem_limit_bytes=VMEM_LIMIT),
        name="experts",
    )(blk_e, n_real, x_sorted, w13, w2)


def _final_kernel(x2_ref, yg_ref, rwt_ref, mod_ref, fg_ref, out_ref):
    gate2 = mod_ref[0, 5:6, :]
    rwt = rwt_ref[0]
    yg = yg_ref[0]
    moe = rwt[:, 0:1] * yg[:, 0:D_MODEL].astype(F32) + rwt[:, 1:2] * yg[:, D_MODEL:2 * D_MODEL].astype(F32)
    x3 = x2_ref[0] + gate2 * moe
    out_ref[0] = _rms(x3, fg_ref[...])


def _final(x2, yg, rwt, mod, fg):
    batch, seq, _ = x2.shape
    tm = TOKEN_TILE
    tok = lambda w: pl.BlockSpec((1, tm, w), lambda b, i: (b, i, 0))
    return pl.pallas_call(
        _final_kernel,
        grid=(batch, seq // tm),
        in_specs=[
            tok(D_MODEL), tok(TOP_K * D_MODEL), tok(LANES),
            pl.BlockSpec((1, N_MOD, D_MODEL), lambda b, i: (b, 0, 0)),
            pl.BlockSpec((1, D_MODEL), lambda b, i: (0, 0)),
        ],
        out_specs=tok(D_MODEL),
        out_shape=jax.ShapeDtypeStruct((batch, seq, D_MODEL), F32),
        compiler_params=pltpu.CompilerParams(
            dimension_semantics=("arbitrary", "arbitrary"), vmem_limit_bytes=VMEM_LIMIT),
        name="final",
    )(x2, yg, rwt, mod, fg)


def _pack_weights(w_in, w_uq, w_ukv, w_br_mla, w_rg, b_rg, w_re, b_re, b_s):
    kr_end = Q_LORA_RANK + KV_LORA_RANK + QK_ROPE_DIM
    w_in_p = jnp.concatenate(
        [w_in[:, :kr_end], jnp.zeros((D_MODEL, _SEG_KR[1] - kr_end), F32), w_in[:, kr_end:]],
        axis=1).astype(BF16)
    pad = HEAD_PAD - QK_DIM
    wq = w_uq.reshape(Q_LORA_RANK, MLA_HEADS, QK_DIM)
    nope, r1, r2 = wq[..., :QK_NOPE_DIM], wq[..., QK_NOPE_DIM:QK_NOPE_DIM + ROPE_HALF], wq[..., QK_NOPE_DIM + ROPE_HALF:]
    zpad = jnp.zeros((Q_LORA_RANK, MLA_HEADS, pad), F32)
    wqa = jnp.concatenate([nope, r1, r2, zpad], axis=-1).reshape(Q_LORA_RANK, -1).astype(BF16)
    wqb = jnp.concatenate([jnp.zeros_like(nope), r2, r1, zpad], axis=-1).reshape(Q_LORA_RANK, -1).astype(BF16)
    wkv = w_ukv.reshape(KV_LORA_RANK, MLA_HEADS, QK_NOPE_DIM + V_HEAD_DIM)
    wkt = wkv[..., :QK_NOPE_DIM].transpose(1, 2, 0).astype(BF16)
    wv = jnp.concatenate([wkv[..., QK_NOPE_DIM:], jnp.zeros((KV_LORA_RANK, MLA_HEADS, HEAD_PAD - V_HEAD_DIM), F32)],
                         axis=-1).reshape(KV_LORA_RANK, -1).astype(BF16)
    vones = jnp.zeros((MLA_HEADS, HEAD_PAD), F32).at[:, V_HEAD_DIM].set(1.0).reshape(1, -1)
    wbm = jnp.concatenate(
        [w_br_mla.reshape(MLA_HEADS, V_HEAD_DIM, D_MODEL),
         jnp.zeros((MLA_HEADS, HEAD_PAD - V_HEAD_DIM, D_MODEL), F32)], axis=1).astype(BF16)
    n_r = N_GROUPS + N_EXPERTS
    wr = jnp.concatenate([w_rg, w_re, jnp.zeros((D_MODEL, LANES - n_r), F32)], axis=1)
    wr_hi = wr.astype(BF16)
    wr_lo = (wr - wr_hi.astype(F32)).astype(BF16)
    rw = jnp.concatenate([wr_hi, wr_lo], axis=1)
    rb = jnp.concatenate([b_rg, b_re, jnp.zeros((LANES - n_r,), F32)]).reshape(1, LANES)
    bs_full = jnp.repeat(b_s.T, SGU_GROUP_DIM, axis=1)
    fcol = (ROPE_THETA ** (-jnp.arange(ROPE_HALF, dtype=F32) / ROPE_HALF)).reshape(ROPE_HALF, 1)
    return w_in_p, wqa, wqb, wkt, wv, vones, wbm, rw, rb, bs_full, fcol


def kernel(x, c, positions, w_ada, b_ada, norm1_g, w_in, q_norm_g, w_uq, kv_norm_g, w_ukv, v_norm_g, v_norm_b, w_s, b_s, w_br_mla, w_br_sgu, w_out, norm2_g, w_rg, b_rg, w_re, b_re, w1, w3, w2, final_g):
    batch, seq, _ = x.shape
    n_tok = batch * seq
    row = lambda v: v.reshape(1, -1)
    for l in range(w_ada.shape[0]):
        (w_in_p, wqa, wqb, wkt, wv, vones, wbm, rw, rb, bs_full, fcol) = _pack_weights(
            w_in[l], w_uq[l], w_ukv[l], w_br_mla[l], w_rg[l], b_rg[l], w_re[l], b_re[l], b_s[l])
        mod = _adaln(c, w_ada[l], b_ada[l]).reshape(batch, N_MOD, D_MODEL)
        q, kt, v, gm, ms = _pre(
            x, mod, positions, row(norm1_g[l]), w_in_p, row(q_norm_g[l]), wqa, wqb, row(kv_norm_g[l]),
            wkt, wv, vones, fcol, row(v_norm_g[l]), row(v_norm_b[l]), w_s[l], bs_full,
            w_br_sgu[l].astype(BF16))
        o = _flash(q, kt, v)
        x2, h2, ri, rwt, cnt = _post(x, o, gm, ms, mod, wbm, w_out[l].astype(BF16), row(norm2_g[l]), rw, rb)

        counts = cnt[0, :N_EXPERTS].astype(jnp.int32)
        padded = ((counts + EXPERT_BLOCK - 1) // EXPERT_BLOCK) * EXPERT_BLOCK
        pad_ends = jnp.cumsum(padded)
        pad_starts = pad_ends - padded
        n_assign = n_tok * TOP_K
        rows_total = ((n_assign + EXPERT_BLOCK - 1) // EXPERT_BLOCK + N_EXPERTS) * EXPERT_BLOCK
        nblk = rows_total // EXPERT_BLOCK
        ri2 = ri.reshape(n_tok, LANES)
        dest = (pad_starts[ri2[:, 0:TOP_K]] + ri2[:, TOP_K:2 * TOP_K]).reshape(-1)
        slot_tok = jnp.zeros((rows_total,), jnp.int32).at[dest].set(
            jnp.arange(n_assign, dtype=jnp.int32) // TOP_K)
        blk_e = jnp.minimum(
            jnp.searchsorted(pad_ends, jnp.arange(nblk, dtype=jnp.int32) * EXPERT_BLOCK, side='right'),
            N_EXPERTS - 1).astype(jnp.int32)
        n_real = (pad_ends[-1:] // EXPERT_BLOCK).astype(jnp.int32)
        x_sorted = h2.reshape(n_tok, D_MODEL)[slot_tok]
        w13 = jnp.concatenate([w1[l], w3[l]], axis=-1).astype(BF16)
        y = _experts(blk_e, n_real, x_sorted, w13, w2[l].astype(BF16))
        yg = y[dest].reshape(batch, seq, TOP_K * D_MODEL)
        x = _final(x2, yg, rwt, mod, row(final_g)) if l == w_ada.shape[0] - 1 else None
    return x
```

```python
import jax
import jax.numpy as jnp
from jax import lax
from jax.experimental import pallas as pl
from jax.experimental.pallas import tpu as pltpu

F32 = jnp.float32
BF16 = jnp.bfloat16

D_MODEL = 1024
MLA_HEADS = 8
QK_NOPE_DIM = 64
QK_ROPE_DIM = 32
ROPE_HALF = QK_ROPE_DIM // 2
QK_DIM = QK_NOPE_DIM + QK_ROPE_DIM
V_HEAD_DIM = 64
Q_LORA_RANK = 256
KV_LORA_RANK = 128
ROPE_THETA = 10000.0
SGU_GROUPS = 8
SGU_GROUP_DIM = 64
SGU_WIDTH = SGU_GROUPS * SGU_GROUP_DIM
SGU_CHUNK = 128
N_GROUPS = 4
EXPERTS_PER_GROUP = 8
N_EXPERTS = N_GROUPS * EXPERTS_PER_GROUP
TOP_K = 2
EXPERT_FF = 512
N_MOD = 6
EPS = 1e-6

LANES = 128
HEAD_PAD = LANES
VMEM_LIMIT = 48 * 1024 * 1024

TOKEN_TILE = 512
ATTN_TILE = 512
ATTN_CHAINS = 4
EXPERT_BLOCK = 256
ADALN_TILE = 512

_SEG_CQ = (0, 256)
_SEG_CKV = (256, 384)
_SEG_KR = (384, 512)
_SEG_U = (512, 1024)
_SEG_V = (1024, 1536)
_SEG_GMLA = (1536, 2560)
_SEG_GSGU = (2560, 3584)
PACKED_IN = 3584

Q_SCALE = (QK_DIM ** -0.5) * 1.4426950408889634

_NT_DIMS = (((1,), (1,)), ((), ()))


def _sigmoid(x):
    return 1.0 / (1.0 + jnp.exp(-x))


def _gelu_tanh(x):
    c = 0.7978845608028654
    return 0.5 * x * (1.0 + jnp.tanh(c * (x + 0.044715 * (x * x * x))))


def _rms(x, g):
    return x * lax.rsqrt(jnp.mean(x * x, axis=-1, keepdims=True) + EPS) * g


def _dot(a, b):
    return jnp.dot(a, b, preferred_element_type=F32)


def _dot_nt(a, b):
    return lax.dot_general(a, b, _NT_DIMS, preferred_element_type=F32)


def _adaln_kernel(ct_ref, w_ref, b_ref, o_ref):
    ct = ct_ref[...]
    s = ct * _sigmoid(ct)
    w = w_ref[...]
    rows = [jnp.sum(s[:, b:b + 1] * w, axis=0, keepdims=True) for b in range(ct.shape[1])]
    o_ref[...] = jnp.concatenate(rows, axis=0) + b_ref[...]


def _adaln(c, w_ada, b_ada):
    batch = c.shape[0]
    width = w_ada.shape[1]
    return pl.pallas_call(
        _adaln_kernel,
        grid=(width // ADALN_TILE,),
        in_specs=[
            pl.BlockSpec((D_MODEL, batch), lambda j: (0, 0)),
            pl.BlockSpec((D_MODEL, ADALN_TILE), lambda j: (0, j)),
            pl.BlockSpec((1, ADALN_TILE), lambda j: (0, j)),
        ],
        out_specs=pl.BlockSpec((batch, ADALN_TILE), lambda j: (0, j)),
        out_shape=jax.ShapeDtypeStruct((batch, width), F32),
        compiler_params=pltpu.CompilerParams(dimension_semantics=("arbitrary",)),
        name="adaln",
    )(c.T, w_ada, b_ada.reshape(1, width))


def _pre_kernel(x_ref, mod_ref, pos_ref, g1_ref, win_ref, qg_ref, wqa_ref, wqb_ref,
                kvg_ref, wk_ref, wv_ref, vones_ref, fcol_ref,
                vng_ref, vnb_ref, ws_ref, bs_ref, wbs_ref,
                qt_out, k_out, vt_out, gm_out, ms_out):
    tm = x_ref.shape[1]
    x = x_ref[0]
    shift1 = mod_ref[0, 0:1, :]
    scale1 = mod_ref[0, 1:2, :]
    h1 = _rms(x, g1_ref[...]) * (1.0 + scale1) + shift1
    hb = h1.astype(BF16)

    def proj(seg):
        return _dot(hb, win_ref[:, seg[0]:seg[1]])

    pos = pos_ref[0].astype(F32)
    ang = fcol_ref[...] * pos
    cos_t = jnp.cos(ang)
    sin_t = jnp.sin(ang)
    pad_rows = HEAD_PAD - QK_DIM

    cqn = _rms(proj(_SEG_CQ), qg_ref[...]).astype(BF16)
    qa_t = _dot_nt(wqa_ref[...], cqn)
    qb_t = _dot_nt(wqb_ref[...], cqn)
    c_tab = jnp.concatenate([jnp.ones((QK_NOPE_DIM, tm), F32), cos_t, cos_t,
                             jnp.ones((pad_rows, tm), F32)], axis=0) * Q_SCALE
    s_tab = jnp.concatenate([jnp.zeros((QK_NOPE_DIM, tm), F32), -sin_t, sin_t,
                             jnp.zeros((pad_rows, tm), F32)], axis=0) * Q_SCALE
    for h in range(MLA_HEADS):
        sl = slice(h * HEAD_PAD, (h + 1) * HEAD_PAD)
        qt_out[0, h, 0] = (qa_t[sl] * c_tab + qb_t[sl] * s_tab).astype(BF16)

    ckvn = _rms(proj(_SEG_CKV), kvg_ref[...]).astype(BF16)
    kr_t = proj(_SEG_KR).T
    x1 = kr_t[0:ROPE_HALF]
    x2 = kr_t[ROPE_HALF:QK_ROPE_DIM]
    k_pe = jnp.concatenate([jnp.zeros((QK_NOPE_DIM, tm), F32), x1 * cos_t - x2 * sin_t,
                            x1 * sin_t + x2 * cos_t, jnp.zeros((pad_rows, tm), F32)], axis=0).T
    k_full = _dot(ckvn, wk_ref[...])
    v_t = _dot_nt(wv_ref[...], ckvn) + vones_ref[...]
    for h in range(MLA_HEADS):
        sl = slice(h * HEAD_PAD, (h + 1) * HEAD_PAD)
        k_out[0, h] = (k_full[:, sl] + k_pe).astype(BF16)
        vt_out[0, h, 0] = v_t[sl].astype(BF16)

    gu = _gelu_tanh(proj(_SEG_U))
    gv = _gelu_tanh(proj(_SEG_V))
    mu = jnp.mean(gv, axis=-1, keepdims=True)
    dv = gv - mu
    var = jnp.mean(dv * dv, axis=-1, keepdims=True)
    vln = (dv * lax.rsqrt(var + EPS) * vng_ref[...] + vnb_ref[...]).astype(BF16)
    row = lax.broadcasted_iota(jnp.int32, (SGU_CHUNK, SGU_CHUNK), 0)
    col = lax.broadcasted_iota(jnp.int32, (SGU_CHUNK, SGU_CHUNK), 1)
    causal = col <= row
    w_tril = [jnp.where(causal, ws_ref[g], 0.0).astype(BF16) for g in range(SGU_GROUPS)]
    lane_grp = lax.broadcasted_iota(jnp.int32, (SGU_CHUNK, SGU_WIDTH), 1) // SGU_GROUP_DIM
    chunks = []
    for c in range(tm // SGU_CHUNK):
        vc = vln[c * SGU_CHUNK:(c + 1) * SGU_CHUNK]
        acc = jnp.zeros((SGU_CHUNK, SGU_WIDTH), F32)
        for g in range(SGU_GROUPS):
            acc = jnp.where(lane_grp == g, _dot(w_tril[g], vc), acc)
        chunks.append(acc + bs_ref[...])
    sgu = (gu * jnp.concatenate(chunks, axis=0)).astype(BF16)
    y_sgu = _dot(sgu, wbs_ref[...])
    ms_out[0] = (_sigmoid(proj(_SEG_GSGU)) * y_sgu).astype(BF16)
    gm_out[0] = _sigmoid(proj(_SEG_GMLA)).astype(BF16)


def _pre(x, mod, positions, g1, w_in_p, qg, wqa_t, wqb_t, kvg, wk, wv_t, vones, fcol,
         vng, vnb, w_s, bs_full, wbs):
    batch, seq, _ = x.shape
    tm = TOKEN_TILE
    assert tm == ATTN_TILE and seq % tm == 0
    nt = seq // tm
    hw = MLA_HEADS * HEAD_PAD

    def const(shape):
        return pl.BlockSpec(shape, lambda b, i: (0,) * len(shape))

    return pl.pallas_call(
        _pre_kernel,
        grid=(batch, nt),
        in_specs=[
            pl.BlockSpec((1, tm, D_MODEL), lambda b, i: (b, i, 0)),
            pl.BlockSpec((1, N_MOD, D_MODEL), lambda b, i: (b, 0, 0)),
            pl.BlockSpec((1, 1, tm), lambda b, i: (b, 0, i)),
            const((1, D_MODEL)),
            const((D_MODEL, PACKED_IN)),
            const((1, Q_LORA_RANK)),
            const((hw, Q_LORA_RANK)),
            const((hw, Q_LORA_RANK)),
            const((1, KV_LORA_RANK)),
            const((KV_LORA_RANK, hw)),
            const((hw, KV_LORA_RANK)),
            const((hw, 1)),
            const((ROPE_HALF, 1)),
            const((1, SGU_WIDTH)),
            const((1, SGU_WIDTH)),
            const((SGU_GROUPS, SGU_CHUNK, SGU_CHUNK)),
            const((SGU_CHUNK, SGU_WIDTH)),
            const((SGU_WIDTH, D_MODEL)),
        ],
        out_specs=[
            pl.BlockSpec((1, MLA_HEADS, 1, HEAD_PAD, tm), lambda b, i: (b, 0, i, 0, 0)),
            pl.BlockSpec((1, MLA_HEADS, tm, HEAD_PAD), lambda b, i: (b, 0, i, 0)),
            pl.BlockSpec((1, MLA_HEADS, 1, HEAD_PAD, tm), lambda b, i: (b, 0, i, 0, 0)),
            pl.BlockSpec((1, tm, D_MODEL), lambda b, i: (b, i, 0)),
            pl.BlockSpec((1, tm, D_MODEL), lambda b, i: (b, i, 0)),
        ],
        out_shape=[
            jax.ShapeDtypeStruct((batch, MLA_HEADS, nt, HEAD_PAD, tm), BF16),
            jax.ShapeDtypeStruct((batch, MLA_HEADS, seq, HEAD_PAD), BF16),
            jax.ShapeDtypeStruct((batch, MLA_HEADS, nt, HEAD_PAD, tm), BF16),
            jax.ShapeDtypeStruct((batch, seq, D_MODEL), BF16),
            jax.ShapeDtypeStruct((batch, seq, D_MODEL), BF16),
        ],
        compiler_params=pltpu.CompilerParams(
            dimension_semantics=("arbitrary", "arbitrary"), vmem_limit_bytes=VMEM_LIMIT),
        name="pre",
    )(x, mod, positions.reshape(batch, 1, seq), g1, w_in_p, qg, wqa_t, wqb_t, kvg, wk, wv_t, vones,
      fcol, vng, vnb, w_s, bs_full, wbs)


def _flash_kernel(qt_ref, k_ref, vt_ref, o_ref, m_ref, acc_ref):
    t = qt_ref.shape[4]
    i = pl.program_id(2)
    m_ref[...] = jnp.full(m_ref.shape, -1e30, F32)
    acc_ref[...] = jnp.zeros(acc_ref.shape, F32)

    def logits(c, j):
        start = pl.multiple_of(j * t, t)
        return _dot(k_ref[0, 0, pl.ds(start, t), :], qt_ref[0, 0, c])

    def update(c, j, s, masked):
        if masked:
            key = lax.broadcasted_iota(jnp.int32, (t, t), 0)
            qry = lax.broadcasted_iota(jnp.int32, (t, t), 1)
            s = jnp.where(key <= qry, s, -jnp.inf)
        m_old = m_ref[c]
        m_new = jnp.maximum(m_old, jnp.max(s, axis=0, keepdims=True))
        p = jnp.exp2(s - m_new).astype(BF16)
        pv = _dot(vt_ref[0, 0, j], p)
        acc_ref[c] = jnp.exp2(m_old - m_new) * acc_ref[c] + pv
        m_ref[c] = m_new

    def body(j, carry):
        s_all = [logits(c, j) for c in range(ATTN_CHAINS)]
        for c in range(ATTN_CHAINS):
            update(c, j, s_all[c], False)
        return carry

    first = ATTN_CHAINS * i
    lax.fori_loop(0, first, body, 0)
    for d in range(ATTN_CHAINS):
        s_all = [logits(c, first + d) for c in range(d, ATTN_CHAINS)]
        for c in range(d, ATTN_CHAINS):
            update(c, first + d, s_all[c - d], c == d)
    for c in range(ATTN_CHAINS):
        acc = acc_ref[c]
        o_ref[0, 0, c * t:(c + 1) * t, :] = (acc / acc[V_HEAD_DIM:V_HEAD_DIM + 1, :]).T.astype(BF16)


def _flash(qt, k, vt):
    batch, heads, nt, _, t = qt.shape
    seq = nt * t
    assert nt % ATTN_CHAINS == 0
    return pl.pallas_call(
        _flash_kernel,
        grid=(batch, heads, nt // ATTN_CHAINS),
        in_specs=[
            pl.BlockSpec((1, 1, ATTN_CHAINS, HEAD_PAD, t), lambda b, h, i: (b, h, i, 0, 0)),
            pl.BlockSpec((1, 1, seq, HEAD_PAD), lambda b, h, i: (b, h, 0, 0)),
            pl.BlockSpec((1, 1, nt, HEAD_PAD, t), lambda b, h, i: (b, h, 0, 0, 0)),
        ],
        out_specs=pl.BlockSpec((1, 1, ATTN_CHAINS * t, HEAD_PAD), lambda b, h, i: (b, h, i, 0)),
        out_shape=jax.ShapeDtypeStruct((batch, heads, seq, HEAD_PAD), BF16),
        scratch_shapes=[pltpu.VMEM((ATTN_CHAINS, 1, t), F32), pltpu.VMEM((ATTN_CHAINS, HEAD_PAD, t), F32)],
        compiler_params=pltpu.CompilerParams(
            dimension_semantics=("arbitrary", "arbitrary", "arbitrary"),
            vmem_limit_bytes=VMEM_LIMIT),
        name="flash",
    )(qt, k, vt)


def _post_kernel(x_ref, o_ref, gm_ref, ms_ref, mod_ref, wbm_ref, wout_ref, g2_ref, rw_ref, rb_ref,
                 x2_out, h2_out, ri_out, rwt_out, cnt_out, carry_ref):
    tm = x_ref.shape[1]

    @pl.when((pl.program_id(0) == 0) & (pl.program_id(1) == 0))
    def _():
        carry_ref[...] = jnp.zeros(carry_ref.shape, F32)

    gate1 = mod_ref[0, 2:3, :]
    shift2 = mod_ref[0, 3:4, :]
    scale2 = mod_ref[0, 4:5, :]
    y_mla = _dot(o_ref[0, 0], wbm_ref[0])
    for h in range(1, MLA_HEADS):
        y_mla = y_mla + _dot(o_ref[0, h], wbm_ref[h])
    merged = gm_ref[0].astype(F32) * y_mla + ms_ref[0].astype(F32)
    x2 = x_ref[0] + gate1 * _dot(merged.astype(BF16), wout_ref[...])
    x2_out[0] = x2
    h2 = _rms(x2, g2_ref[...]) * (1.0 + scale2) + shift2
    h2_out[0] = h2.astype(BF16)

    hi = h2.astype(BF16)
    lo = (h2 - hi.astype(F32)).astype(BF16)
    a = _dot(hi, rw_ref[...])
    b = _dot(lo, rw_ref[:, 0:LANES])
    logits = a[:, 0:LANES] + a[:, LANES:2 * LANES] + b + rb_ref[...]

    lane = lax.broadcasted_iota(jnp.int32, (tm, LANES), 1).astype(F32)
    big = float(LANES)
    ninf = -jnp.inf
    gl = jnp.where(lane < N_GROUPS, logits, ninf)
    gmax = jnp.max(gl, axis=-1, keepdims=True)
    grp = jnp.min(jnp.where(gl == gmax, lane, big), axis=-1, keepdims=True)
    p_grp = 1.0 / jnp.sum(jnp.exp(gl - gmax), axis=-1, keepdims=True)
    first = N_GROUPS + EXPERTS_PER_GROUP * grp
    el = jnp.where((lane >= first) & (lane < first + EXPERTS_PER_GROUP), logits, ninf)
    m1 = jnp.max(el, axis=-1, keepdims=True)
    i1 = jnp.min(jnp.where(el == m1, lane, big), axis=-1, keepdims=True)
    el2 = jnp.where(lane == i1, ninf, el)
    m2 = jnp.max(el2, axis=-1, keepdims=True)
    i2 = jnp.min(jnp.where(el2 == m2, lane, big), axis=-1, keepdims=True)
    tt = jnp.exp(m2 - m1)
    w0 = p_grp / (1.0 + tt)
    w1 = p_grp * tt / (1.0 + tt)
    e0 = i1 - N_GROUPS
    e1 = i2 - N_GROUPS

    onehot = jnp.where((lane == e0) | (lane == e1), 1.0, 0.0)
    row = lax.broadcasted_iota(jnp.int32, (tm, tm), 0)
    col = lax.broadcasted_iota(jnp.int32, (tm, tm), 1)
    lower = jnp.where(col < row, 1.0, 0.0).astype(BF16)
    before = _dot(lower, onehot.astype(BF16)) + carry_ref[...]
    r0 = jnp.sum(jnp.where(lane == e0, before, 0.0), axis=-1, keepdims=True)
    r1 = jnp.sum(jnp.where(lane == e1, before, 0.0), axis=-1, keepdims=True)
    carry_ref[...] = carry_ref[...] + jnp.sum(onehot, axis=0, keepdims=True)
    cnt_out[...] = carry_ref[...]

    ri = jnp.where(lane == 0, e0, jnp.where(lane == 1, e1, jnp.where(lane == 2, r0, jnp.where(lane == 3, r1, 0.0))))
    ri_out[0] = ri.astype(jnp.int32)
    rwt_out[0] = jnp.where(lane == 0, w0, jnp.where(lane == 1, w1, 0.0))


def _post(x, o, gm, ms, mod, wbm, wout, g2, rw, rb):
    batch, seq, _ = x.shape
    tm = TOKEN_TILE

    def const(shape):
        return pl.BlockSpec(shape, lambda b, i: (0,) * len(shape))

    tok = lambda w: pl.BlockSpec((1, tm, w), lambda b, i: (b, i, 0))
    return pl.pallas_call(
        _post_kernel,
        grid=(batch, seq // tm),
        in_specs=[
            tok(D_MODEL),
            pl.BlockSpec((1, MLA_HEADS, tm, HEAD_PAD), lambda b, i: (b, 0, i, 0)),
            tok(D_MODEL),
            tok(D_MODEL),
            pl.BlockSpec((1, N_MOD, D_MODEL), lambda b, i: (b, 0, 0)),
            const((MLA_HEADS, HEAD_PAD, D_MODEL)),
            const((D_MODEL, D_MODEL)),
            const((1, D_MODEL)),
            const((D_MODEL, 2 * LANES)),
            const((1, LANES)),
        ],
        out_specs=[tok(D_MODEL), tok(D_MODEL), tok(LANES), tok(LANES), const((1, LANES))],
        out_shape=[
            jax.ShapeDtypeStruct((batch, seq, D_MODEL), F32),
            jax.ShapeDtypeStruct((batch, seq, D_MODEL), BF16),
            jax.ShapeDtypeStruct((batch, seq, LANES), jnp.int32),
            jax.ShapeDtypeStruct((batch, seq, LANES), F32),
            jax.ShapeDtypeStruct((1, LANES), F32),
        ],
        scratch_shapes=[pltpu.VMEM((1, LANES), F32)],
        compiler_params=pltpu.CompilerParams(
            dimension_semantics=("arbitrary", "arbitrary"), vmem_limit_bytes=VMEM_LIMIT),
        name="post",
    )(x, o, gm, ms, mod, wbm, wout, g2, rw, rb)


def _expert_kernel(be_ref, nr_ref, x_ref, w13_ref, w2_ref, y_ref):
    j = pl.program_id(0)

    @pl.when(j < nr_ref[0])
    def _():
        h = _dot(x_ref[...], w13_ref[0])
        a = h[:, 0:EXPERT_FF]
        act = (a * _sigmoid(a) * h[:, EXPERT_FF:2 * EXPERT_FF]).astype(BF16)
        y_ref[...] = _dot(act, w2_ref[0]).astype(BF16)

    @pl.when(j >= nr_ref[0])
    def _():
        y_ref[...] = jnp.zeros(y_ref.shape, BF16)


def _experts(blk_e, n_real, x_sorted, w13, w2):
    rows = x_sorted.shape[0]
    nblk = rows // EXPERT_BLOCK
    last = lambda j, be, nr: jnp.minimum(j, nr[0] - 1)
    return pl.pallas_call(
        _expert_kernel,
        grid_spec=pltpu.PrefetchScalarGridSpec(
            num_scalar_prefetch=2,
            grid=(nblk,),
            in_specs=[
                pl.BlockSpec((EXPERT_BLOCK, D_MODEL), lambda j, be, nr: (last(j, be, nr), 0)),
                pl.BlockSpec((1, D_MODEL, 2 * EXPERT_FF), lambda j, be, nr: (be[last(j, be, nr)], 0, 0)),
                pl.BlockSpec((1, EXPERT_FF, D_MODEL), lambda j, be, nr: (be[last(j, be, nr)], 0, 0)),
            ],
            out_specs=pl.BlockSpec((EXPERT_BLOCK, D_MODEL), lambda j, be, nr: (j, 0)),
        ),
        out_shape=jax.ShapeDtypeStruct((rows, D_MODEL), BF16),
        compiler_params=pltpu.CompilerParams(
            dimension_semantics=("arbitrary",), vmem_limit_bytes=VMEM_LIMIT),
        name="experts",
    )(blk_e, n_real, x_sorted, w13, w2)


def _final_kernel(x2_ref, yg_ref, rwt_ref, mod_ref, fg_ref, out_ref):
    gate2 = mod_ref[0, 5:6, :]
    rwt = rwt_ref[0]
    yg = yg_ref[0]
    moe = rwt[:, 0:1] * yg[:, 0:D_MODEL].astype(F32) + rwt[:, 1:2] * yg[:, D_MODEL:2 * D_MODEL].astype(F32)
    x3 = x2_ref[0] + gate2 * moe
    out_ref[0] = _rms(x3, fg_ref[...])


def _final(x2, yg, rwt, mod, fg):
    batch, seq, _ = x2.shape
    tm = TOKEN_TILE
    tok = lambda w: pl.BlockSpec((1, tm, w), lambda b, i: (b, i, 0))
    return pl.pallas_call(
        _final_kernel,
        grid=(batch, seq // tm),
        in_specs=[
            tok(D_MODEL), tok(TOP_K * D_MODEL), tok(LANES),
            pl.BlockSpec((1, N_MOD, D_MODEL), lambda b, i: (b, 0, 0)),
            pl.BlockSpec((1, D_MODEL), lambda b, i: (0, 0)),
        ],
        out_specs=tok(D_MODEL),
        out_shape=jax.ShapeDtypeStruct((batch, seq, D_MODEL), F32),
        compiler_params=pltpu.CompilerParams(
            dimension_semantics=("arbitrary", "arbitrary"), vmem_limit_bytes=VMEM_LIMIT),
        name="final",
    )(x2, yg, rwt, mod, fg)


def _pack_weights(w_in, w_uq, w_ukv, w_br_mla, w_rg, b_rg, w_re, b_re, b_s):
    kr_end = Q_LORA_RANK + KV_LORA_RANK + QK_ROPE_DIM
    w_in_p = jnp.concatenate(
        [w_in[:, :kr_end], jnp.zeros((D_MODEL, _SEG_KR[1] - kr_end), F32), w_in[:, kr_end:]],
        axis=1).astype(BF16)
    pad = HEAD_PAD - QK_DIM
    hw = MLA_HEADS * HEAD_PAD
    wq = w_uq.reshape(Q_LORA_RANK, MLA_HEADS, QK_DIM)
    nope, r1, r2 = wq[..., :QK_NOPE_DIM], wq[..., QK_NOPE_DIM:QK_NOPE_DIM + ROPE_HALF], wq[..., QK_NOPE_DIM + ROPE_HALF:]
    zpad = jnp.zeros((Q_LORA_RANK, MLA_HEADS, pad), F32)
    wqa_t = jnp.concatenate([nope, r1, r2, zpad], axis=-1).reshape(Q_LORA_RANK, hw).T.astype(BF16)
    wqb_t = jnp.concatenate([jnp.zeros_like(nope), r2, r1, zpad], axis=-1).reshape(Q_LORA_RANK, hw).T.astype(BF16)
    wkv = w_ukv.reshape(KV_LORA_RANK, MLA_HEADS, QK_NOPE_DIM + V_HEAD_DIM)
    wk = jnp.concatenate([wkv[..., :QK_NOPE_DIM], jnp.zeros((KV_LORA_RANK, MLA_HEADS, HEAD_PAD - QK_NOPE_DIM), F32)],
                         axis=-1).reshape(KV_LORA_RANK, hw).astype(BF16)
    wv_t = jnp.concatenate([wkv[..., QK_NOPE_DIM:], jnp.zeros((KV_LORA_RANK, MLA_HEADS, HEAD_PAD - V_HEAD_DIM), F32)],
                           axis=-1).reshape(KV_LORA_RANK, hw).T.astype(BF16)
    vones = jnp.zeros((MLA_HEADS, HEAD_PAD), F32).at[:, V_HEAD_DIM].set(1.0).reshape(hw, 1)
    wbm = jnp.concatenate(
        [w_br_mla.reshape(MLA_HEADS, V_HEAD_DIM, D_MODEL),
         jnp.zeros((MLA_HEADS, HEAD_PAD - V_HEAD_DIM, D_MODEL), F32)], axis=1).astype(BF16)
    n_r = N_GROUPS + N_EXPERTS
    wr = jnp.concatenate([w_rg, w_re, jnp.zeros((D_MODEL, LANES - n_r), F32)], axis=1)
    wr_hi = wr.astype(BF16)
    wr_lo = (wr - wr_hi.astype(F32)).astype(BF16)
    rw = jnp.concatenate([wr_hi, wr_lo], axis=1)
    rb = jnp.concatenate([b_rg, b_re, jnp.zeros((LANES - n_r,), F32)]).reshape(1, LANES)
    bs_full = jnp.repeat(b_s.T, SGU_GROUP_DIM, axis=1)
    fcol = (ROPE_THETA ** (-jnp.arange(ROPE_HALF, dtype=F32) / ROPE_HALF)).reshape(ROPE_HALF, 1)
    return w_in_p, wqa_t, wqb_t, wk, wv_t, vones, wbm, rw, rb, bs_full, fcol


def kernel(x, c, positions, w_ada, b_ada, norm1_g, w_in, q_norm_g, w_uq, kv_norm_g, w_ukv, v_norm_g, v_norm_b, w_s, b_s, w_br_mla, w_br_sgu, w_out, norm2_g, w_rg, b_rg, w_re, b_re, w1, w3, w2, final_g):
    batch, seq, _ = x.shape
    n_tok = batch * seq
    assert w_ada.shape[0] == 1, "single-layer block"
    l = 0
    row = lambda v: v.reshape(1, -1)
    (w_in_p, wqa_t, wqb_t, wk, wv_t, vones, wbm, rw, rb, bs_full, fcol) = _pack_weights(
        w_in[l], w_uq[l], w_ukv[l], w_br_mla[l], w_rg[l], b_rg[l], w_re[l], b_re[l], b_s[l])
    mod = _adaln(c, w_ada[l], b_ada[l]).reshape(batch, N_MOD, D_MODEL)
    qt, k, vt, gm, ms = _pre(
        x, mod, positions, row(norm1_g[l]), w_in_p, row(q_norm_g[l]), wqa_t, wqb_t, row(kv_norm_g[l]),
        wk, wv_t, vones, fcol, row(v_norm_g[l]), row(v_norm_b[l]), w_s[l], bs_full,
        w_br_sgu[l].astype(BF16))
    o = _flash(qt, k, vt)
    x2, h2, ri, rwt, cnt = _post(x, o, gm, ms, mod, wbm, w_out[l].astype(BF16), row(norm2_g[l]), rw, rb)

    counts = cnt[0, :N_EXPERTS].astype(jnp.int32)
    padded = ((counts + EXPERT_BLOCK - 1) // EXPERT_BLOCK) * EXPERT_BLOCK
    pad_ends = jnp.cumsum(padded)
    pad_starts = pad_ends - padded
    n_assign = n_tok * TOP_K
    rows_total = ((n_assign + EXPERT_BLOCK - 1) // EXPERT_BLOCK + N_EXPERTS) * EXPERT_BLOCK
    nblk = rows_total // EXPERT_BLOCK
    ri2 = ri.reshape(n_tok, LANES)
    dest = (pad_starts[ri2[:, 0:TOP_K]] + ri2[:, TOP_K:2 * TOP_K]).reshape(-1)
    slot_tok = jnp.zeros((rows_total,), jnp.int32).at[dest].set(
        jnp.arange(n_assign, dtype=jnp.int32) // TOP_K)
    blk_start = jnp.arange(nblk, dtype=jnp.int32) * EXPERT_BLOCK
    blk_e = jnp.minimum(jnp.sum((pad_ends[None, :] <= blk_start[:, None]).astype(jnp.int32), axis=1),
                        N_EXPERTS - 1)
    n_real = pad_ends[-1:] // EXPERT_BLOCK
    x_sorted = h2.reshape(n_tok, D_MODEL)[slot_tok]
    w13 = jnp.concatenate([w1[l], w3[l]], axis=-1).astype(BF16)
    y = _experts(blk_e, n_real, x_sorted, w13, w2[l].astype(BF16))
    yg = y[dest].reshape(batch, seq, TOP_K * D_MODEL)
    return _final(x2, yg, rwt, mod, row(final_g))
```

```python
import jax
import jax.numpy as jnp
from jax import lax
from jax.experimental import pallas as pl
from jax.experimental.pallas import tpu as pltpu

F32 = jnp.float32
BF16 = jnp.bfloat16

D_MODEL = 1024
MLA_HEADS = 8
QK_NOPE_DIM = 64
QK_ROPE_DIM = 32
ROPE_HALF = QK_ROPE_DIM // 2
QK_DIM = QK_NOPE_DIM + QK_ROPE_DIM
V_HEAD_DIM = 64
Q_LORA_RANK = 256
KV_LORA_RANK = 128
ROPE_THETA = 10000.0
SGU_GROUPS = 8
SGU_GROUP_DIM = 64
SGU_WIDTH = SGU_GROUPS * SGU_GROUP_DIM
SGU_CHUNK = 128
N_GROUPS = 4
EXPERTS_PER_GROUP = 8
N_EXPERTS = N_GROUPS * EXPERTS_PER_GROUP
TOP_K = 2
EXPERT_FF = 512
N_MOD = 6
EPS = 1e-6

LANES = 128
HEAD_PAD = LANES
VMEM_LIMIT = 48 * 1024 * 1024

TOKEN_TILE = 512
ATTN_TILE = 512
ATTN_CHAINS = 4
EXPERT_BLOCK = 256
ADALN_TILE = 512
ROUTE_ROWS = 8

_SEG_CQ = (0, 256)
_SEG_CKV = (256, 384)
_SEG_KR = (384, 512)
_SEG_U = (512, 1024)
_SEG_V = (1024, 1536)
_SEG_GMLA = (1536, 2560)
_SEG_GSGU = (2560, 3584)
PACKED_IN = 3584

Q_SCALE = (QK_DIM ** -0.5) * 1.4426950408889634

_NT_DIMS = (((1,), (1,)), ((), ()))


def _sigmoid(x):
    return 1.0 / (1.0 + jnp.exp(-x))


def _gelu_tanh(x):
    c = 0.7978845608028654
    return 0.5 * x * (1.0 + jnp.tanh(c * (x + 0.044715 * (x * x * x))))


def _rms(x, g):
    return x * lax.rsqrt(jnp.mean(x * x, axis=-1, keepdims=True) + EPS) * g


def _dot(a, b):
    return jnp.dot(a, b, preferred_element_type=F32)


def _dot_nt(a, b):
    return lax.dot_general(a, b, _NT_DIMS, preferred_element_type=F32)


def _adaln_kernel(ct_ref, w_ref, b_ref, o_ref):
    ct = ct_ref[...]
    s = ct * _sigmoid(ct)
    w = w_ref[...]
    rows = [jnp.sum(s[:, b:b + 1] * w, axis=0, keepdims=True) for b in range(ct.shape[1])]
    o_ref[...] = jnp.concatenate(rows, axis=0) + b_ref[...]


def _adaln(c, w_ada, b_ada):
    batch = c.shape[0]
    width = w_ada.shape[1]
    return pl.pallas_call(
        _adaln_kernel,
        grid=(width // ADALN_TILE,),
        in_specs=[
            pl.BlockSpec((D_MODEL, batch), lambda j: (0, 0)),
            pl.BlockSpec((D_MODEL, ADALN_TILE), lambda j: (0, j)),
            pl.BlockSpec((1, ADALN_TILE), lambda j: (0, j)),
        ],
        out_specs=pl.BlockSpec((batch, ADALN_TILE), lambda j: (0, j)),
        out_shape=jax.ShapeDtypeStruct((batch, width), F32),
        compiler_params=pltpu.CompilerParams(dimension_semantics=("arbitrary",)),
        name="adaln",
    )(c.T, w_ada, b_ada.reshape(1, width))


def _pre_kernel(x_ref, mod_ref, pos_ref, g1_ref, win_ref, qg_ref, wqa_ref, wqb_ref,
                kvg_ref, wk_ref, wv_ref, vones_ref, fcol_ref,
                vng_ref, vnb_ref, ws_ref, bs_ref, wbs_ref,
                qt_out, k_out, vt_out, gm_out, ms_out):
    tm = x_ref.shape[1]
    x = x_ref[0]
    shift1 = mod_ref[0, 0:1, :]
    scale1 = mod_ref[0, 1:2, :]
    h1 = _rms(x, g1_ref[...]) * (1.0 + scale1) + shift1
    hb = h1.astype(BF16)

    def proj(seg):
        return _dot(hb, win_ref[:, seg[0]:seg[1]])

    pos = pos_ref[0].astype(F32)
    ang = fcol_ref[...] * pos
    cos_t = jnp.cos(ang)
    sin_t = jnp.sin(ang)
    pad_rows = HEAD_PAD - QK_DIM

    cqn = _rms(proj(_SEG_CQ), qg_ref[...]).astype(BF16)
    qa_t = _dot_nt(wqa_ref[...], cqn)
    qb_t = _dot_nt(wqb_ref[...], cqn)
    c_tab = jnp.concatenate([jnp.ones((QK_NOPE_DIM, tm), F32), cos_t, cos_t,
                             jnp.ones((pad_rows, tm), F32)], axis=0) * Q_SCALE
    s_tab = jnp.concatenate([jnp.zeros((QK_NOPE_DIM, tm), F32), -sin_t, sin_t,
                             jnp.zeros((pad_rows, tm), F32)], axis=0) * Q_SCALE
    for h in range(MLA_HEADS):
        sl = slice(h * HEAD_PAD, (h + 1) * HEAD_PAD)
        qt_out[0, h, 0] = (qa_t[sl] * c_tab + qb_t[sl] * s_tab).astype(BF16)

    ckvn = _rms(proj(_SEG_CKV), kvg_ref[...]).astype(BF16)
    kr_t = proj(_SEG_KR).T
    x1 = kr_t[0:ROPE_HALF]
    x2 = kr_t[ROPE_HALF:QK_ROPE_DIM]
    k_pe = jnp.concatenate([jnp.zeros((QK_NOPE_DIM, tm), F32), x1 * cos_t - x2 * sin_t,
                            x1 * sin_t + x2 * cos_t, jnp.zeros((pad_rows, tm), F32)], axis=0).T
    k_full = _dot(ckvn, wk_ref[...])
    v_t = _dot_nt(wv_ref[...], ckvn) + vones_ref[...]
    for h in range(MLA_HEADS):
        sl = slice(h * HEAD_PAD, (h + 1) * HEAD_PAD)
        k_out[0, h] = (k_full[:, sl] + k_pe).astype(BF16)
        vt_out[0, h, 0] = v_t[sl].astype(BF16)

    gu = _gelu_tanh(proj(_SEG_U))
    gv = _gelu_tanh(proj(_SEG_V))
    mu = jnp.mean(gv, axis=-1, keepdims=True)
    dv = gv - mu
    var = jnp.mean(dv * dv, axis=-1, keepdims=True)
    vln = (dv * lax.rsqrt(var + EPS) * vng_ref[...] + vnb_ref[...]).astype(BF16)
    row = lax.broadcasted_iota(jnp.int32, (SGU_CHUNK, SGU_CHUNK), 0)
    col = lax.broadcasted_iota(jnp.int32, (SGU_CHUNK, SGU_CHUNK), 1)
    causal = col <= row
    w_tril = [jnp.where(causal, ws_ref[g], 0.0).astype(BF16) for g in range(SGU_GROUPS)]
    lane_grp = lax.broadcasted_iota(jnp.int32, (SGU_CHUNK, SGU_WIDTH), 1) // SGU_GROUP_DIM
    chunks = []
    for c in range(tm // SGU_CHUNK):
        vc = vln[c * SGU_CHUNK:(c + 1) * SGU_CHUNK]
        acc = jnp.zeros((SGU_CHUNK, SGU_WIDTH), F32)
        for g in range(SGU_GROUPS):
            acc = jnp.where(lane_grp == g, _dot(w_tril[g], vc), acc)
        chunks.append(acc + bs_ref[...])
    sgu = (gu * jnp.concatenate(chunks, axis=0)).astype(BF16)
    y_sgu = _dot(sgu, wbs_ref[...])
    ms_out[0] = (_sigmoid(proj(_SEG_GSGU)) * y_sgu).astype(BF16)
    gm_out[0] = _sigmoid(proj(_SEG_GMLA)).astype(BF16)


def _pre(x, mod, positions, g1, w_in_p, qg, wqa_t, wqb_t, kvg, wk, wv_t, vones, fcol,
         vng, vnb, w_s, bs_full, wbs):
    batch, seq, _ = x.shape
    tm = TOKEN_TILE
    assert tm == ATTN_TILE and seq % tm == 0
    nt = seq // tm
    hw = MLA_HEADS * HEAD_PAD

    def const(shape):
        return pl.BlockSpec(shape, lambda b, i: (0,) * len(shape))

    return pl.pallas_call(
        _pre_kernel,
        grid=(batch, nt),
        in_specs=[
            pl.BlockSpec((1, tm, D_MODEL), lambda b, i: (b, i, 0)),
            pl.BlockSpec((1, N_MOD, D_MODEL), lambda b, i: (b, 0, 0)),
            pl.BlockSpec((1, 1, tm), lambda b, i: (b, 0, i)),
            const((1, D_MODEL)),
            const((D_MODEL, PACKED_IN)),
            const((1, Q_LORA_RANK)),
            const((hw, Q_LORA_RANK)),
            const((hw, Q_LORA_RANK)),
            const((1, KV_LORA_RANK)),
            const((KV_LORA_RANK, hw)),
            const((hw, KV_LORA_RANK)),
            const((hw, 1)),
            const((ROPE_HALF, 1)),
            const((1, SGU_WIDTH)),
            const((1, SGU_WIDTH)),
            const((SGU_GROUPS, SGU_CHUNK, SGU_CHUNK)),
            const((SGU_CHUNK, SGU_WIDTH)),
            const((SGU_WIDTH, D_MODEL)),
        ],
        out_specs=[
            pl.BlockSpec((1, MLA_HEADS, 1, HEAD_PAD, tm), lambda b, i: (b, 0, i, 0, 0)),
            pl.BlockSpec((1, MLA_HEADS, tm, HEAD_PAD), lambda b, i: (b, 0, i, 0)),
            pl.BlockSpec((1, MLA_HEADS, 1, HEAD_PAD, tm), lambda b, i: (b, 0, i, 0, 0)),
            pl.BlockSpec((1, tm, D_MODEL), lambda b, i: (b, i, 0)),
            pl.BlockSpec((1, tm, D_MODEL), lambda b, i: (b, i, 0)),
        ],
        out_shape=[
            jax.ShapeDtypeStruct((batch, MLA_HEADS, nt, HEAD_PAD, tm), BF16),
            jax.ShapeDtypeStruct((batch, MLA_HEADS, seq, HEAD_PAD), BF16),
            jax.ShapeDtypeStruct((batch, MLA_HEADS, nt, HEAD_PAD, tm), BF16),
            jax.ShapeDtypeStruct((batch, seq, D_MODEL), BF16),
            jax.ShapeDtypeStruct((batch, seq, D_MODEL), BF16),
        ],
        compiler_params=pltpu.CompilerParams(
            dimension_semantics=("arbitrary", "arbitrary"), vmem_limit_bytes=VMEM_LIMIT),
        name="pre",
    )(x, mod, positions.reshape(batch, 1, seq), g1, w_in_p, qg, wqa_t, wqb_t, kvg, wk, wv_t, vones,
      fcol, vng, vnb, w_s, bs_full, wbs)


def _flash_kernel(qt_ref, k_ref, vt_ref, o_ref, m_ref, acc_ref):
    t = qt_ref.shape[4]
    i = pl.program_id(2)
    m_ref[...] = jnp.full(m_ref.shape, -1e30, F32)
    acc_ref[...] = jnp.zeros(acc_ref.shape, F32)

    def logits(c, j):
        start = pl.multiple_of(j * t, t)
        return _dot(k_ref[0, 0, pl.ds(start, t), :], qt_ref[0, 0, c])

    def update(c, j, s, masked):
        if masked:
            key = lax.broadcasted_iota(jnp.int32, (t, t), 0)
            qry = lax.broadcasted_iota(jnp.int32, (t, t), 1)
            s = jnp.where(key <= qry, s, -jnp.inf)
        m_old = m_ref[c]
        m_new = jnp.maximum(m_old, jnp.max(s, axis=0, keepdims=True))
        p = jnp.exp2(s - m_new).astype(BF16)
        pv = _dot(vt_ref[0, 0, j], p)
        acc_ref[c] = jnp.exp2(m_old - m_new) * acc_ref[c] + pv
        m_ref[c] = m_new

    def body(j, carry):
        s_all = [logits(c, j) for c in range(ATTN_CHAINS)]
        for c in range(ATTN_CHAINS):
            update(c, j, s_all[c], False)
        return carry

    first = ATTN_CHAINS * i
    lax.fori_loop(0, first, body, 0)
    for d in range(ATTN_CHAINS):
        s_all = [logits(c, first + d) for c in range(d, ATTN_CHAINS)]
        for c in range(d, ATTN_CHAINS):
            update(c, first + d, s_all[c - d], c == d)
    for c in range(ATTN_CHAINS):
        acc = acc_ref[c]
        o_ref[0, 0, c * t:(c + 1) * t, :] = (acc / acc[V_HEAD_DIM:V_HEAD_DIM + 1, :]).T.astype(BF16)


def _flash(qt, k, vt):
    batch, heads, nt, _, t = qt.shape
    seq = nt * t
    assert nt % ATTN_CHAINS == 0
    return pl.pallas_call(
        _flash_kernel,
        grid=(batch, heads, nt // ATTN_CHAINS),
        in_specs=[
            pl.BlockSpec((1, 1, ATTN_CHAINS, HEAD_PAD, t), lambda b, h, i: (b, h, i, 0, 0)),
            pl.BlockSpec((1, 1, seq, HEAD_PAD), lambda b, h, i: (b, h, 0, 0)),
            pl.BlockSpec((1, 1, nt, HEAD_PAD, t), lambda b, h, i: (b, h, 0, 0, 0)),
        ],
        out_specs=pl.BlockSpec((1, 1, ATTN_CHAINS * t, HEAD_PAD), lambda b, h, i: (b, h, i, 0)),
        out_shape=jax.ShapeDtypeStruct((batch, heads, seq, HEAD_PAD), BF16),
        scratch_shapes=[pltpu.VMEM((ATTN_CHAINS, 1, t), F32), pltpu.VMEM((ATTN_CHAINS, HEAD_PAD, t), F32)],
        compiler_params=pltpu.CompilerParams(
            dimension_semantics=("arbitrary", "arbitrary", "arbitrary"),
            vmem_limit_bytes=VMEM_LIMIT),
        name="flash",
    )(qt, k, vt)


def _post_kernel(x_ref, o_ref, gm_ref, ms_ref, mod_ref, wbm_ref, wout_ref, g2_ref, rw_ref, rb_ref,
                 x2_out, h2_out, rt_out, rwt_out, cnt_out, carry_ref):
    tm = x_ref.shape[1]

    @pl.when((pl.program_id(0) == 0) & (pl.program_id(1) == 0))
    def _():
        carry_ref[...] = jnp.zeros(carry_ref.shape, F32)

    gate1 = mod_ref[0, 2:3, :]
    shift2 = mod_ref[0, 3:4, :]
    scale2 = mod_ref[0, 4:5, :]
    y_mla = _dot(o_ref[0, 0], wbm_ref[0])
    for h in range(1, MLA_HEADS):
        y_mla = y_mla + _dot(o_ref[0, h], wbm_ref[h])
    merged = gm_ref[0].astype(F32) * y_mla + ms_ref[0].astype(F32)
    x2 = x_ref[0] + gate1 * _dot(merged.astype(BF16), wout_ref[...])
    x2_out[0] = x2
    h2 = _rms(x2, g2_ref[...]) * (1.0 + scale2) + shift2
    h2_out[0] = h2.astype(BF16)

    hi = h2.astype(BF16)
    lo = (h2 - hi.astype(F32)).astype(BF16)
    a = _dot(hi, rw_ref[...])
    b = _dot(lo, rw_ref[:, 0:LANES])
    logits = a[:, 0:LANES] + a[:, LANES:2 * LANES] + b + rb_ref[...]

    lane = lax.broadcasted_iota(jnp.int32, (tm, LANES), 1).astype(F32)
    big = float(LANES)
    ninf = -jnp.inf
    gl = jnp.where(lane < N_GROUPS, logits, ninf)
    gmax = jnp.max(gl, axis=-1, keepdims=True)
    grp = jnp.min(jnp.where(gl == gmax, lane, big), axis=-1, keepdims=True)
    p_grp = 1.0 / jnp.sum(jnp.exp(gl - gmax), axis=-1, keepdims=True)
    first = N_GROUPS + EXPERTS_PER_GROUP * grp
    el = jnp.where((lane >= first) & (lane < first + EXPERTS_PER_GROUP), logits, ninf)
    m1 = jnp.max(el, axis=-1, keepdims=True)
    i1 = jnp.min(jnp.where(el == m1, lane, big), axis=-1, keepdims=True)
    el2 = jnp.where(lane == i1, ninf, el)
    m2 = jnp.max(el2, axis=-1, keepdims=True)
    i2 = jnp.min(jnp.where(el2 == m2, lane, big), axis=-1, keepdims=True)
    tt = jnp.exp(m2 - m1)
    w0 = p_grp / (1.0 + tt)
    w1 = p_grp * tt / (1.0 + tt)
    e0 = i1 - N_GROUPS
    e1 = i2 - N_GROUPS

    onehot = jnp.where((lane == e0) | (lane == e1), 1.0, 0.0)
    row = lax.broadcasted_iota(jnp.int32, (tm, tm), 0)
    col = lax.broadcasted_iota(jnp.int32, (tm, tm), 1)
    lower = jnp.where(col < row, 1.0, 0.0).astype(BF16)
    before = _dot(lower, onehot.astype(BF16)) + carry_ref[...]
    r0 = jnp.sum(jnp.where(lane == e0, before, 0.0), axis=-1, keepdims=True)
    r1 = jnp.sum(jnp.where(lane == e1, before, 0.0), axis=-1, keepdims=True)
    carry_ref[...] = carry_ref[...] + jnp.sum(onehot, axis=0, keepdims=True)
    cnt_out[...] = carry_ref[...]

    ri = jnp.where(lane == 0, e0, jnp.where(lane == 1, e1, jnp.where(lane == 2, r0, jnp.where(lane == 3, r1, 0.0))))
    rt_out[0, 0] = ri.T[0:ROUTE_ROWS]
    rwt_out[0] = jnp.where(lane == 0, w0, jnp.where(lane == 1, w1, 0.0))


def _post(x, o, gm, ms, mod, wbm, wout, g2, rw, rb):
    batch, seq, _ = x.shape
    tm = TOKEN_TILE

    def const(shape):
        return pl.BlockSpec(shape, lambda b, i: (0,) * len(shape))

    tok = lambda w: pl.BlockSpec((1, tm, w), lambda b, i: (b, i, 0))
    return pl.pallas_call(
        _post_kernel,
        grid=(batch, seq // tm),
        in_specs=[
            tok(D_MODEL),
            pl.BlockSpec((1, MLA_HEADS, tm, HEAD_PAD), lambda b, i: (b, 0, i, 0)),
            tok(D_MODEL),
            tok(D_MODEL),
            pl.BlockSpec((1, N_MOD, D_MODEL), lambda b, i: (b, 0, 0)),
            const((MLA_HEADS, HEAD_PAD, D_MODEL)),
            const((D_MODEL, D_MODEL)),
            const((1, D_MODEL)),
            const((D_MODEL, 2 * LANES)),
            const((1, LANES)),
        ],
        out_specs=[tok(D_MODEL), tok(D_MODEL),
                   pl.BlockSpec((1, 1, ROUTE_ROWS, tm), lambda b, i: (b, i, 0, 0)),
                   tok(LANES), const((1, LANES))],
        out_shape=[
            jax.ShapeDtypeStruct((batch, seq, D_MODEL), F32),
            jax.ShapeDtypeStruct((batch, seq, D_MODEL), BF16),
            jax.ShapeDtypeStruct((batch, seq // tm, ROUTE_ROWS, tm), F32),
            jax.ShapeDtypeStruct((batch, seq, LANES), F32),
            jax.ShapeDtypeStruct((1, LANES), F32),
        ],
        scratch_shapes=[pltpu.VMEM((1, LANES), F32)],
        compiler_params=pltpu.CompilerParams(
            dimension_semantics=("arbitrary", "arbitrary"), vmem_limit_bytes=VMEM_LIMIT),
        name="post",
    )(x, o, gm, ms, mod, wbm, wout, g2, rw, rb)


def _expert_kernel(be_ref, nr_ref, x_ref, w1_ref, w3_ref, w2_ref, y_ref):
    j = pl.program_id(0)

    @pl.when(j < nr_ref[0])
    def _():
        x = x_ref[...]
        a = _dot(x, w1_ref[0, 0].astype(BF16))
        b = _dot(x, w3_ref[0, 0].astype(BF16))
        act = (a * _sigmoid(a) * b).astype(BF16)
        y_ref[...] = _dot(act, w2_ref[0, 0].astype(BF16)).astype(BF16)

    @pl.when(j >= nr_ref[0])
    def _():
        y_ref[...] = jnp.zeros(y_ref.shape, BF16)


def _experts(blk_e, n_real, x_sorted, w1, w3, w2):
    rows = x_sorted.shape[0]
    nblk = rows // EXPERT_BLOCK
    last = lambda j, be, nr: jnp.minimum(j, nr[0] - 1)
    w_map = lambda j, be, nr: (0, be[last(j, be, nr)], 0, 0)
    return pl.pallas_call(
        _expert_kernel,
        grid_spec=pltpu.PrefetchScalarGridSpec(
            num_scalar_prefetch=2,
            grid=(nblk,),
            in_specs=[
                pl.BlockSpec((EXPERT_BLOCK, D_MODEL), lambda j, be, nr: (last(j, be, nr), 0)),
                pl.BlockSpec((1, 1, D_MODEL, EXPERT_FF), w_map),
                pl.BlockSpec((1, 1, D_MODEL, EXPERT_FF), w_map),
                pl.BlockSpec((1, 1, EXPERT_FF, D_MODEL), w_map),
            ],
            out_specs=pl.BlockSpec((EXPERT_BLOCK, D_MODEL), lambda j, be, nr: (j, 0)),
        ),
        out_shape=jax.ShapeDtypeStruct((rows, D_MODEL), BF16),
        compiler_params=pltpu.CompilerParams(
            dimension_semantics=("arbitrary",), vmem_limit_bytes=VMEM_LIMIT),
        name="experts",
    )(blk_e, n_real, x_sorted, w1, w3, w2)


def _final_kernel(x2_ref, y0_ref, y1_ref, rwt_ref, mod_ref, fg_ref, out_ref):
    gate2 = mod_ref[0, 5:6, :]
    rwt = rwt_ref[0]
    moe = rwt[:, 0:1] * y0_ref[0].astype(F32) + rwt[:, 1:2] * y1_ref[0].astype(F32)
    x3 = x2_ref[0] + gate2 * moe
    out_ref[0] = _rms(x3, fg_ref[...])


def _final(x2, y0, y1, rwt, mod, fg):
    batch, seq, _ = x2.shape
    tm = TOKEN_TILE
    tok = lambda w: pl.BlockSpec((1, tm, w), lambda b, i: (b, i, 0))
    return pl.pallas_call(
        _final_kernel,
        grid=(batch, seq // tm),
        in_specs=[
            tok(D_MODEL), tok(D_MODEL), tok(D_MODEL), tok(LANES),
            pl.BlockSpec((1, N_MOD, D_MODEL), lambda b, i: (b, 0, 0)),
            pl.BlockSpec((1, D_MODEL), lambda b, i: (0, 0)),
        ],
        out_specs=tok(D_MODEL),
        out_shape=jax.ShapeDtypeStruct((batch, seq, D_MODEL), F32),
        compiler_params=pltpu.CompilerParams(
            dimension_semantics=("arbitrary", "arbitrary"), vmem_limit_bytes=VMEM_LIMIT),
        name="final",
    )(x2, y0, y1, rwt, mod, fg)


def _pack_weights(w_in, w_uq, w_ukv, w_br_mla, w_rg, b_rg, w_re, b_re, b_s):
    kr_end = Q_LORA_RANK + KV_LORA_RANK + QK_ROPE_DIM
    w_in_p = jnp.concatenate(
        [w_in[:, :kr_end], jnp.zeros((D_MODEL, _SEG_KR[1] - kr_end), F32), w_in[:, kr_end:]],
        axis=1).astype(BF16)
    pad = HEAD_PAD - QK_DIM
    hw = MLA_HEADS * HEAD_PAD
    wq = w_uq.reshape(Q_LORA_RANK, MLA_HEADS, QK_DIM)
    nope, r1, r2 = wq[..., :QK_NOPE_DIM], wq[..., QK_NOPE_DIM:QK_NOPE_DIM + ROPE_HALF], wq[..., QK_NOPE_DIM + ROPE_HALF:]
    zpad = jnp.zeros((Q_LORA_RANK, MLA_HEADS, pad), F32)
    wqa_t = jnp.concatenate([nope, r1, r2, zpad], axis=-1).reshape(Q_LORA_RANK, hw).T.astype(BF16)
    wqb_t = jnp.concatenate([jnp.zeros_like(nope), r2, r1, zpad], axis=-1).reshape(Q_LORA_RANK, hw).T.astype(BF16)
    wkv = w_ukv.reshape(KV_LORA_RANK, MLA_HEADS, QK_NOPE_DIM + V_HEAD_DIM)
    wk = jnp.concatenate([wkv[..., :QK_NOPE_DIM], jnp.zeros((KV_LORA_RANK, MLA_HEADS, HEAD_PAD - QK_NOPE_DIM), F32)],
                         axis=-1).reshape(KV_LORA_RANK, hw).astype(BF16)
    wv_t = jnp.concatenate([wkv[..., QK_NOPE_DIM:], jnp.zeros((KV_LORA_RANK, MLA_HEADS, HEAD_PAD - V_HEAD_DIM), F32)],
                           axis=-1).reshape(KV_LORA_RANK, hw).T.astype(BF16)
    vones = jnp.zeros((MLA_HEADS, HEAD_PAD), F32).at[:, V_HEAD_DIM].set(1.0).reshape(hw, 1)
    wbm = jnp.concatenate(
        [w_br_mla.reshape(MLA_HEADS, V_HEAD_DIM, D_MODEL),
         jnp.zeros((MLA_HEADS, HEAD_PAD - V_HEAD_DIM, D_MODEL), F32)], axis=1).astype(BF16)
    n_r = N_GROUPS + N_EXPERTS
    wr = jnp.concatenate([w_rg, w_re, jnp.zeros((D_MODEL, LANES - n_r), F32)], axis=1)
    wr_hi = wr.astype(BF16)
    wr_lo = (wr - wr_hi.astype(F32)).astype(BF16)
    rw = jnp.concatenate([wr_hi, wr_lo], axis=1)
    rb = jnp.concatenate([b_rg, b_re, jnp.zeros((LANES - n_r,), F32)]).reshape(1, LANES)
    bs_full = jnp.repeat(b_s.T, SGU_GROUP_DIM, axis=1)
    fcol = (ROPE_THETA ** (-jnp.arange(ROPE_HALF, dtype=F32) / ROPE_HALF)).reshape(ROPE_HALF, 1)
    return w_in_p, wqa_t, wqb_t, wk, wv_t, vones, wbm, rw, rb, bs_full, fcol


def kernel(x, c, positions, w_ada, b_ada, norm1_g, w_in, q_norm_g, w_uq, kv_norm_g, w_ukv, v_norm_g, v_norm_b, w_s, b_s, w_br_mla, w_br_sgu, w_out, norm2_g, w_rg, b_rg, w_re, b_re, w1, w3, w2, final_g):
    batch, seq, _ = x.shape
    n_tok = batch * seq
    assert w_ada.shape[0] == 1, "single-layer block"
    l = 0
    row = lambda v: v.reshape(1, -1)
    (w_in_p, wqa_t, wqb_t, wk, wv_t, vones, wbm, rw, rb, bs_full, fcol) = _pack_weights(
        w_in[l], w_uq[l], w_ukv[l], w_br_mla[l], w_rg[l], b_rg[l], w_re[l], b_re[l], b_s[l])
    mod = _adaln(c, w_ada[l], b_ada[l]).reshape(batch, N_MOD, D_MODEL)
    qt, k, vt, gm, ms = _pre(
        x, mod, positions, row(norm1_g[l]), w_in_p, row(q_norm_g[l]), wqa_t, wqb_t, row(kv_norm_g[l]),
        wk, wv_t, vones, fcol, row(v_norm_g[l]), row(v_norm_b[l]), w_s[l], bs_full,
        w_br_sgu[l].astype(BF16))
    o = _flash(qt, k, vt)
    x2, h2, rt, rwt, cnt = _post(x, o, gm, ms, mod, wbm, w_out[l].astype(BF16), row(norm2_g[l]), rw, rb)

    counts = cnt[0, :N_EXPERTS].astype(jnp.int32)
    padded = ((counts + EXPERT_BLOCK - 1) // EXPERT_BLOCK) * EXPERT_BLOCK
    pad_ends = jnp.cumsum(padded)
    pad_starts = pad_ends - padded
    n_assign = n_tok * TOP_K
    rows_total = ((n_assign + EXPERT_BLOCK - 1) // EXPERT_BLOCK + N_EXPERTS) * EXPERT_BLOCK
    nblk = rows_total // EXPERT_BLOCK
    rti = rt.astype(jnp.int32)
    expert_ids = jnp.arange(N_EXPERTS, dtype=jnp.int32)
    dests = []
    for kk in range(TOP_K):
        e_k = rti[:, :, kk, :].reshape(n_tok)
        r_k = rti[:, :, TOP_K + kk, :].reshape(n_tok)
        start_k = jnp.sum(jnp.where(e_k[:, None] == expert_ids[None, :], pad_starts[None, :], 0), axis=1)
        dests.append(start_k + r_k)
    tok_ids = jnp.arange(n_tok, dtype=jnp.int32)
    slot_tok = (jnp.arange(rows_total, dtype=jnp.int32) % n_tok).at[jnp.concatenate(dests)].set(
        jnp.concatenate([tok_ids] * TOP_K))
    blk_start = jnp.arange(nblk, dtype=jnp.int32) * EXPERT_BLOCK
    blk_e = jnp.minimum(jnp.sum((pad_ends[None, :] <= blk_start[:, None]).astype(jnp.int32), axis=1),
                        N_EXPERTS - 1)
    n_real = pad_ends[-1:] // EXPERT_BLOCK
    x_sorted = h2.reshape(n_tok, D_MODEL)[slot_tok]
    y = _experts(blk_e, n_real, x_sorted, w1, w3, w2)
    y0, y1 = [y[d].reshape(batch, seq, D_MODEL) for d in dests]
    return _final(x2, y0, y1, rwt, mod, row(final_g))
```

```python
import jax
import jax.numpy as jnp
from jax import lax
from jax.experimental import pallas as pl
from jax.experimental.pallas import tpu as pltpu

F32 = jnp.float32
BF16 = jnp.bfloat16

D_MODEL = 1024
MLA_HEADS = 8
QK_NOPE_DIM = 64
QK_ROPE_DIM = 32
ROPE_HALF = QK_ROPE_DIM // 2
QK_DIM = QK_NOPE_DIM + QK_ROPE_DIM
V_HEAD_DIM = 64
Q_LORA_RANK = 256
KV_LORA_RANK = 128
ROPE_THETA = 10000.0
SGU_GROUPS = 8
SGU_GROUP_DIM = 64
SGU_WIDTH = SGU_GROUPS * SGU_GROUP_DIM
SGU_CHUNK = 128
N_GROUPS = 4
EXPERTS_PER_GROUP = 8
N_EXPERTS = N_GROUPS * EXPERTS_PER_GROUP
TOP_K = 2
EXPERT_FF = 512
N_MOD = 6
EPS = 1e-6

LANES = 128
HEAD_PAD = LANES
VMEM_LIMIT = 48 * 1024 * 1024

TOKEN_TILE = 512
ATTN_TILE = 512
ATTN_CHAINS = 4
EXPERT_BLOCK = 256
ADALN_TILE = 512
ROUTE_ROWS = 8

_SEG_CQ = (0, 256)
_SEG_CKV = (256, 384)
_SEG_KR = (384, 512)
_SEG_U = (512, 1024)
_SEG_V = (1024, 1536)
_SEG_GMLA = (1536, 2560)
_SEG_GSGU = (2560, 3584)
PACKED_IN = 3584

Q_SCALE = (QK_DIM ** -0.5) * 1.4426950408889634

_NT_DIMS = (((1,), (1,)), ((), ()))


def _sigmoid(x):
    return 1.0 / (1.0 + jnp.exp(-x))


def _gelu_tanh(x):
    c = 0.7978845608028654
    return 0.5 * x * (1.0 + jnp.tanh(c * (x + 0.044715 * (x * x * x))))


def _rms(x, g):
    return x * lax.rsqrt(jnp.mean(x * x, axis=-1, keepdims=True) + EPS) * g


def _dot(a, b):
    return jnp.dot(a, b, preferred_element_type=F32)


def _dot_nt(a, b):
    return lax.dot_general(a, b, _NT_DIMS, preferred_element_type=F32)


def _adaln_kernel(ct_ref, w_ref, b_ref, o_ref):
    ct = ct_ref[...]
    s = ct * _sigmoid(ct)
    w = w_ref[...]
    rows = [jnp.sum(s[:, b:b + 1] * w, axis=0, keepdims=True) for b in range(ct.shape[1])]
    o_ref[...] = jnp.concatenate(rows, axis=0) + b_ref[...]


def _adaln(c, w_ada, b_ada):
    batch = c.shape[0]
    width = w_ada.shape[1]
    return pl.pallas_call(
        _adaln_kernel,
        grid=(width // ADALN_TILE,),
        in_specs=[
            pl.BlockSpec((D_MODEL, batch), lambda j: (0, 0)),
            pl.BlockSpec((D_MODEL, ADALN_TILE), lambda j: (0, j)),
            pl.BlockSpec((1, ADALN_TILE), lambda j: (0, j)),
        ],
        out_specs=pl.BlockSpec((batch, ADALN_TILE), lambda j: (0, j)),
        out_shape=jax.ShapeDtypeStruct((batch, width), F32),
        compiler_params=pltpu.CompilerParams(dimension_semantics=("arbitrary",)),
        name="adaln",
    )(c.T, w_ada, b_ada.reshape(1, width))


def _pre_kernel(x_ref, mod_ref, pos_ref, g1_ref, win_ref, qg_ref, wqa_ref, wqb_ref,
                kvg_ref, wk_ref, wv_ref, vones_ref, fcol_ref,
                vng_ref, vnb_ref, ws_ref, bs_ref, wbs_ref,
                qt_out, k_out, vt_out, gm_out, ms_out):
    tm = x_ref.shape[1]
    x = x_ref[0]
    shift1 = mod_ref[0, 0:1, :]
    scale1 = mod_ref[0, 1:2, :]
    h1 = _rms(x, g1_ref[...]) * (1.0 + scale1) + shift1
    hb = h1.astype(BF16)

    def proj(seg):
        return _dot(hb, win_ref[:, seg[0]:seg[1]])

    pos = pos_ref[0].astype(F32)
    ang = fcol_ref[...] * pos
    cos_t = jnp.cos(ang)
    sin_t = jnp.sin(ang)
    pad_rows = HEAD_PAD - QK_DIM

    cqn = _rms(proj(_SEG_CQ), qg_ref[...]).astype(BF16)
    qa_t = _dot_nt(wqa_ref[...], cqn)
    qb_t = _dot_nt(wqb_ref[...], cqn)
    c_tab = jnp.concatenate([jnp.ones((QK_NOPE_DIM, tm), F32), cos_t, cos_t,
                             jnp.ones((pad_rows, tm), F32)], axis=0) * Q_SCALE
    s_tab = jnp.concatenate([jnp.zeros((QK_NOPE_DIM, tm), F32), -sin_t, sin_t,
                             jnp.zeros((pad_rows, tm), F32)], axis=0) * Q_SCALE
    for h in range(MLA_HEADS):
        sl = slice(h * HEAD_PAD, (h + 1) * HEAD_PAD)
        qt_out[0, h, 0] = (qa_t[sl] * c_tab + qb_t[sl] * s_tab).astype(BF16)

    ckvn = _rms(proj(_SEG_CKV), kvg_ref[...]).astype(BF16)
    kr_t = proj(_SEG_KR).T
    x1 = kr_t[0:ROPE_HALF]
    x2 = kr_t[ROPE_HALF:QK_ROPE_DIM]
    k_pe = jnp.concatenate([jnp.zeros((QK_NOPE_DIM, tm), F32), x1 * cos_t - x2 * sin_t,
                            x1 * sin_t + x2 * cos_t, jnp.zeros((pad_rows, tm), F32)], axis=0).T
    k_full = _dot(ckvn, wk_ref[...])
    v_t = _dot_nt(wv_ref[...], ckvn) + vones_ref[...]
    for h in range(MLA_HEADS):
        sl = slice(h * HEAD_PAD, (h + 1) * HEAD_PAD)
        k_out[0, h] = (k_full[:, sl] + k_pe).astype(BF16)
        vt_out[0, h, 0] = v_t[sl].astype(BF16)

    gu = _gelu_tanh(proj(_SEG_U))
    gv = _gelu_tanh(proj(_SEG_V))
    mu = jnp.mean(gv, axis=-1, keepdims=True)
    dv = gv - mu
    var = jnp.mean(dv * dv, axis=-1, keepdims=True)
    vln = (dv * lax.rsqrt(var + EPS) * vng_ref[...] + vnb_ref[...]).astype(BF16)
    row = lax.broadcasted_iota(jnp.int32, (SGU_CHUNK, SGU_CHUNK), 0)
    col = lax.broadcasted_iota(jnp.int32, (SGU_CHUNK, SGU_CHUNK), 1)
    causal = col <= row
    w_tril = [jnp.where(causal, ws_ref[g], 0.0).astype(BF16) for g in range(SGU_GROUPS)]
    lane_grp = lax.broadcasted_iota(jnp.int32, (SGU_CHUNK, SGU_WIDTH), 1) // SGU_GROUP_DIM
    chunks = []
    for c in range(tm // SGU_CHUNK):
        vc = vln[c * SGU_CHUNK:(c + 1) * SGU_CHUNK]
        acc = jnp.zeros((SGU_CHUNK, SGU_WIDTH), F32)
        for g in range(SGU_GROUPS):
            acc = jnp.where(lane_grp == g, _dot(w_tril[g], vc), acc)
        chunks.append(acc + bs_ref[...])
    sgu = (gu * jnp.concatenate(chunks, axis=0)).astype(BF16)
    y_sgu = _dot(sgu, wbs_ref[...])
    ms_out[0] = (_sigmoid(proj(_SEG_GSGU)) * y_sgu).astype(BF16)
    gm_out[0] = _sigmoid(proj(_SEG_GMLA)).astype(BF16)


def _pre(x, mod, positions, g1, w_in_p, qg, wqa_t, wqb_t, kvg, wk, wv_t, vones, fcol,
         vng, vnb, w_s, bs_full, wbs):
    batch, seq, _ = x.shape
    tm = TOKEN_TILE
    assert tm == ATTN_TILE and seq % tm == 0
    nt = seq // tm
    hw = MLA_HEADS * HEAD_PAD

    def const(shape):
        return pl.BlockSpec(shape, lambda b, i: (0,) * len(shape))

    return pl.pallas_call(
        _pre_kernel,
        grid=(batch, nt),
        in_specs=[
            pl.BlockSpec((1, tm, D_MODEL), lambda b, i: (b, i, 0)),
            pl.BlockSpec((1, N_MOD, D_MODEL), lambda b, i: (b, 0, 0)),
            pl.BlockSpec((1, 1, tm), lambda b, i: (b, 0, i)),
            const((1, D_MODEL)),
            const((D_MODEL, PACKED_IN)),
            const((1, Q_LORA_RANK)),
            const((hw, Q_LORA_RANK)),
            const((hw, Q_LORA_RANK)),
            const((1, KV_LORA_RANK)),
            const((KV_LORA_RANK, hw)),
            const((hw, KV_LORA_RANK)),
            const((hw, 1)),
            const((ROPE_HALF, 1)),
            const((1, SGU_WIDTH)),
            const((1, SGU_WIDTH)),
            const((SGU_GROUPS, SGU_CHUNK, SGU_CHUNK)),
            const((SGU_CHUNK, SGU_WIDTH)),
            const((SGU_WIDTH, D_MODEL)),
        ],
        out_specs=[
            pl.BlockSpec((1, MLA_HEADS, 1, HEAD_PAD, tm), lambda b, i: (b, 0, i, 0, 0)),
            pl.BlockSpec((1, MLA_HEADS, tm, HEAD_PAD), lambda b, i: (b, 0, i, 0)),
            pl.BlockSpec((1, MLA_HEADS, 1, HEAD_PAD, tm), lambda b, i: (b, 0, i, 0, 0)),
            pl.BlockSpec((1, tm, D_MODEL), lambda b, i: (b, i, 0)),
            pl.BlockSpec((1, tm, D_MODEL), lambda b, i: (b, i, 0)),
        ],
        out_shape=[
            jax.ShapeDtypeStruct((batch, MLA_HEADS, nt, HEAD_PAD, tm), BF16),
            jax.ShapeDtypeStruct((batch, MLA_HEADS, seq, HEAD_PAD), BF16),
            jax.ShapeDtypeStruct((batch, MLA_HEADS, nt, HEAD_PAD, tm), BF16),
            jax.ShapeDtypeStruct((batch, seq, D_MODEL), BF16),
            jax.ShapeDtypeStruct((batch, seq, D_MODEL), BF16),
        ],
        compiler_params=pltpu.CompilerParams(
            dimension_semantics=("arbitrary", "arbitrary"), vmem_limit_bytes=VMEM_LIMIT),
        name="pre",
    )(x, mod, positions.reshape(batch, 1, seq), g1, w_in_p, qg, wqa_t, wqb_t, kvg, wk, wv_t, vones,
      fcol, vng, vnb, w_s, bs_full, wbs)


def _flash_kernel(qt_ref, k_ref, vt_ref, o_ref, m_ref, acc_ref, s0_ref, s1_ref):
    t = qt_ref.shape[4]
    i = pl.program_id(2)
    m_ref[...] = jnp.full(m_ref.shape, -1e30, F32)
    acc_ref[...] = jnp.zeros(acc_ref.shape, F32)

    def logits(c, j):
        start = pl.multiple_of(j * t, t)
        return _dot(k_ref[0, 0, pl.ds(start, t), :], qt_ref[0, 0, c])

    def update(c, j, s_ref, masked):
        def read():
            s = s_ref[c]
            if masked:
                key = lax.broadcasted_iota(jnp.int32, (t, t), 0)
                qry = lax.broadcasted_iota(jnp.int32, (t, t), 1)
                s = jnp.where(key <= qry, s, -jnp.inf)
            return s
        m_old = m_ref[c]
        m_new = jnp.maximum(m_old, jnp.max(read(), axis=0, keepdims=True))
        p = jnp.exp2(read() - m_new).astype(BF16)
        pv = _dot(vt_ref[0, 0, j], p)
        acc_ref[c] = jnp.exp2(m_old - m_new) * acc_ref[c] + pv
        m_ref[c] = m_new

    bufs = (s0_ref, s1_ref)
    chains = list(range(ATTN_CHAINS))
    last = ATTN_CHAINS - 1
    first = ATTN_CHAINS * i

    for c in chains:
        s0_ref[c] = logits(c, 0)
    for c in chains[:last]:
        s1_ref[c] = logits(c, 1)

    def body(u, carry):
        base = ATTN_CHAINS * u
        s1_ref[last] = logits(last, base + 1)
        for kk in range(ATTN_CHAINS):
            for c in chains:
                update(c, base + kk, bufs[kk % 2], False)
                if not (kk == ATTN_CHAINS - 1 and c == last):
                    bufs[kk % 2][c] = logits(c, base + kk + 2)
        return carry

    lax.fori_loop(0, i, body, 0)
    s1_ref[last] = logits(last, first + 1)
    for d in range(ATTN_CHAINS):
        for c in chains[d:]:
            update(c, first + d, bufs[d % 2], c == d)
            if c >= d + 2:
                bufs[d % 2][c] = logits(c, first + d + 2)
    for c in range(ATTN_CHAINS):
        acc = acc_ref[c]
        o_ref[0, 0, c * t:(c + 1) * t, :] = (acc / acc[V_HEAD_DIM:V_HEAD_DIM + 1, :]).T.astype(BF16)


def _flash(qt, k, vt):
    batch, heads, nt, _, t = qt.shape
    seq = nt * t
    assert nt % ATTN_CHAINS == 0 and ATTN_CHAINS % 2 == 0
    return pl.pallas_call(
        _flash_kernel,
        grid=(batch, heads, nt // ATTN_CHAINS),
        in_specs=[
            pl.BlockSpec((1, 1, ATTN_CHAINS, HEAD_PAD, t), lambda b, h, i: (b, h, i, 0, 0)),
            pl.BlockSpec((1, 1, seq, HEAD_PAD), lambda b, h, i: (b, h, 0, 0)),
            pl.BlockSpec((1, 1, nt, HEAD_PAD, t), lambda b, h, i: (b, h, 0, 0, 0)),
        ],
        out_specs=pl.BlockSpec((1, 1, ATTN_CHAINS * t, HEAD_PAD), lambda b, h, i: (b, h, i, 0)),
        out_shape=jax.ShapeDtypeStruct((batch, heads, seq, HEAD_PAD), BF16),
        scratch_shapes=[pltpu.VMEM((ATTN_CHAINS, 1, t), F32), pltpu.VMEM((ATTN_CHAINS, HEAD_PAD, t), F32),
                        pltpu.VMEM((ATTN_CHAINS, t, t), F32), pltpu.VMEM((ATTN_CHAINS, t, t), F32)],
        compiler_params=pltpu.CompilerParams(
            dimension_semantics=("arbitrary", "arbitrary", "arbitrary"),
            vmem_limit_bytes=VMEM_LIMIT),
        name="flash",
    )(qt, k, vt)


def _post_kernel(x_ref, o_ref, gm_ref, ms_ref, mod_ref, wbm_ref, wout_ref, g2_ref, rw_ref, rb_ref,
                 x2_out, h2_out, rt_out, rwt_out, cnt_out, carry_ref):
    tm = x_ref.shape[1]

    @pl.when((pl.program_id(0) == 0) & (pl.program_id(1) == 0))
    def _():
        carry_ref[...] = jnp.zeros(carry_ref.shape, F32)

    gate1 = mod_ref[0, 2:3, :]
    shift2 = mod_ref[0, 3:4, :]
    scale2 = mod_ref[0, 4:5, :]
    y_mla = _dot(o_ref[0, 0], wbm_ref[0])
    for h in range(1, MLA_HEADS):
        y_mla = y_mla + _dot(o_ref[0, h], wbm_ref[h])
    merged = gm_ref[0].astype(F32) * y_mla + ms_ref[0].astype(F32)
    x2 = x_ref[0] + gate1 * _dot(merged.astype(BF16), wout_ref[...])
    x2_out[0] = x2
    h2 = _rms(x2, g2_ref[...]) * (1.0 + scale2) + shift2
    h2_out[0] = h2.astype(BF16)

    hi = h2.astype(BF16)
    lo = (h2 - hi.astype(F32)).astype(BF16)
    a = _dot(hi, rw_ref[...])
    b = _dot(lo, rw_ref[:, 0:LANES])
    logits = a[:, 0:LANES] + a[:, LANES:2 * LANES] + b + rb_ref[...]

    lane = lax.broadcasted_iota(jnp.int32, (tm, LANES), 1).astype(F32)
    big = float(LANES)
    ninf = -jnp.inf
    gl = jnp.where(lane < N_GROUPS, logits, ninf)
    gmax = jnp.max(gl, axis=-1, keepdims=True)
    grp = jnp.min(jnp.where(gl == gmax, lane, big), axis=-1, keepdims=True)
    p_grp = 1.0 / jnp.sum(jnp.exp(gl - gmax), axis=-1, keepdims=True)
    first = N_GROUPS + EXPERTS_PER_GROUP * grp
    el = jnp.where((lane >= first) & (lane < first + EXPERTS_PER_GROUP), logits, ninf)
    m1 = jnp.max(el, axis=-1, keepdims=True)
    i1 = jnp.min(jnp.where(el == m1, lane, big), axis=-1, keepdims=True)
    el2 = jnp.where(lane == i1, ninf, el)
    m2 = jnp.max(el2, axis=-1, keepdims=True)
    i2 = jnp.min(jnp.where(el2 == m2, lane, big), axis=-1, keepdims=True)
    tt = jnp.exp(m2 - m1)
    w0 = p_grp / (1.0 + tt)
    w1 = p_grp * tt / (1.0 + tt)
    e0 = i1 - N_GROUPS
    e1 = i2 - N_GROUPS

    onehot = jnp.where((lane == e0) | (lane == e1), 1.0, 0.0)
    row = lax.broadcasted_iota(jnp.int32, (tm, tm), 0)
    col = lax.broadcasted_iota(jnp.int32, (tm, tm), 1)
    lower = jnp.where(col < row, 1.0, 0.0).astype(BF16)
    before = _dot(lower, onehot.astype(BF16)) + carry_ref[...]
    r0 = jnp.sum(jnp.where(lane == e0, before, 0.0), axis=-1, keepdims=True)
    r1 = jnp.sum(jnp.where(lane == e1, before, 0.0), axis=-1, keepdims=True)
    carry_ref[...] = carry_ref[...] + jnp.sum(onehot, axis=0, keepdims=True)
    cnt_out[...] = carry_ref[...]

    ri = jnp.where(lane == 0, e0, jnp.where(lane == 1, e1, jnp.where(lane == 2, r0, jnp.where(lane == 3, r1, 0.0))))
    rt_out[0, 0] = ri.T[0:ROUTE_ROWS]
    rwt_out[0] = jnp.where(lane == 0, w0, jnp.where(lane == 1, w1, 0.0))


def _post(x, o, gm, ms, mod, wbm, wout, g2, rw, rb):
    batch, seq, _ = x.shape
    tm = TOKEN_TILE

    def const(shape):
        return pl.BlockSpec(shape, lambda b, i: (0,) * len(shape))

    tok = lambda w: pl.BlockSpec((1, tm, w), lambda b, i: (b, i, 0))
    return pl.pallas_call(
        _post_kernel,
        grid=(batch, seq // tm),
        in_specs=[
            tok(D_MODEL),
            pl.BlockSpec((1, MLA_HEADS, tm, HEAD_PAD), lambda b, i: (b, 0, i, 0)),
            tok(D_MODEL),
            tok(D_MODEL),
            pl.BlockSpec((1, N_MOD, D_MODEL), lambda b, i: (b, 0, 0)),
            const((MLA_HEADS, HEAD_PAD, D_MODEL)),
            const((D_MODEL, D_MODEL)),
            const((1, D_MODEL)),
            const((D_MODEL, 2 * LANES)),
            const((1, LANES)),
        ],
        out_specs=[tok(D_MODEL), tok(D_MODEL),
                   pl.BlockSpec((1, 1, ROUTE_ROWS, tm), lambda b, i: (b, i, 0, 0)),
                   tok(LANES), const((1, LANES))],
        out_shape=[
            jax.ShapeDtypeStruct((batch, seq, D_MODEL), F32),
            jax.ShapeDtypeStruct((batch, seq, D_MODEL), BF16),
            jax.ShapeDtypeStruct((batch, seq // tm, ROUTE_ROWS, tm), F32),
            jax.ShapeDtypeStruct((batch, seq, LANES), F32),
            jax.ShapeDtypeStruct((1, LANES), F32),
        ],
        scratch_shapes=[pltpu.VMEM((1, LANES), F32)],
        compiler_params=pltpu.CompilerParams(
            dimension_semantics=("arbitrary", "arbitrary"), vmem_limit_bytes=VMEM_LIMIT),
        name="post",
    )(x, o, gm, ms, mod, wbm, wout, g2, rw, rb)


def _expert_kernel(be_ref, nr_ref, x_ref, w1_ref, w3_ref, w2_ref, y_ref):
    j = pl.program_id(0)

    @pl.when(j < nr_ref[0])
    def _():
        x = x_ref[...]
        a = _dot(x, w1_ref[0, 0].astype(BF16))
        b = _dot(x, w3_ref[0, 0].astype(BF16))
        act = (a * _sigmoid(a) * b).astype(BF16)
        y_ref[...] = _dot(act, w2_ref[0, 0].astype(BF16)).astype(BF16)

    @pl.when(j >= nr_ref[0])
    def _():
        y_ref[...] = jnp.zeros(y_ref.shape, BF16)


def _experts(blk_e, n_real, x_sorted, w1, w3, w2):
    rows = x_sorted.shape[0]
    nblk = rows // EXPERT_BLOCK
    last = lambda j, be, nr: jnp.minimum(j, nr[0] - 1)
    w_map = lambda j, be, nr: (0, be[last(j, be, nr)], 0, 0)
    return pl.pallas_call(
        _expert_kernel,
        grid_spec=pltpu.PrefetchScalarGridSpec(
            num_scalar_prefetch=2,
            grid=(nblk,),
            in_specs=[
                pl.BlockSpec((EXPERT_BLOCK, D_MODEL), lambda j, be, nr: (last(j, be, nr), 0)),
                pl.BlockSpec((1, 1, D_MODEL, EXPERT_FF), w_map),
                pl.BlockSpec((1, 1, D_MODEL, EXPERT_FF), w_map),
                pl.BlockSpec((1, 1, EXPERT_FF, D_MODEL), w_map),
            ],
            out_specs=pl.BlockSpec((EXPERT_BLOCK, D_MODEL), lambda j, be, nr: (j, 0)),
        ),
        out_shape=jax.ShapeDtypeStruct((rows, D_MODEL), BF16),
        compiler_params=pltpu.CompilerParams(
            dimension_semantics=("arbitrary",), vmem_limit_bytes=VMEM_LIMIT),
        name="experts",
    )(blk_e, n_real, x_sorted, w1, w3, w2)


def _final_kernel(x2_ref, y0_ref, y1_ref, rwt_ref, mod_ref, fg_ref, out_ref):
    gate2 = mod_ref[0, 5:6, :]
    rwt = rwt_ref[0]
    moe = rwt[:, 0:1] * y0_ref[0].astype(F32) + rwt[:, 1:2] * y1_ref[0].astype(F32)
    x3 = x2_ref[0] + gate2 * moe
    out_ref[0] = _rms(x3, fg_ref[...])


def _final(x2, y0, y1, rwt, mod, fg):
    batch, seq, _ = x2.shape
    tm = TOKEN_TILE
    tok = lambda w: pl.BlockSpec((1, tm, w), lambda b, i: (b, i, 0))
    return pl.pallas_call(
        _final_kernel,
        grid=(batch, seq // tm),
        in_specs=[
            tok(D_MODEL), tok(D_MODEL), tok(D_MODEL), tok(LANES),
            pl.BlockSpec((1, N_MOD, D_MODEL), lambda b, i: (b, 0, 0)),
            pl.BlockSpec((1, D_MODEL), lambda b, i: (0, 0)),
        ],
        out_specs=tok(D_MODEL),
        out_shape=jax.ShapeDtypeStruct((batch, seq, D_MODEL), F32),
        compiler_params=pltpu.CompilerParams(
            dimension_semantics=("arbitrary", "arbitrary"), vmem_limit_bytes=VMEM_LIMIT),
        name="final",
    )(x2, y0, y1, rwt, mod, fg)


def _pack_weights(w_in, w_uq, w_ukv, w_br_mla, w_rg, b_rg, w_re, b_re, b_s):
    kr_end = Q_LORA_RANK + KV_LORA_RANK + QK_ROPE_DIM
    w_in_p = jnp.concatenate(
        [w_in[:, :kr_end], jnp.zeros((D_MODEL, _SEG_KR[1] - kr_end), F32), w_in[:, kr_end:]],
        axis=1).astype(BF16)
    pad = HEAD_PAD - QK_DIM
    hw = MLA_HEADS * HEAD_PAD
    wq = w_uq.reshape(Q_LORA_RANK, MLA_HEADS, QK_DIM)
    nope, r1, r2 = wq[..., :QK_NOPE_DIM], wq[..., QK_NOPE_DIM:QK_NOPE_DIM + ROPE_HALF], wq[..., QK_NOPE_DIM + ROPE_HALF:]
    zpad = jnp.zeros((Q_LORA_RANK, MLA_HEADS, pad), F32)
    wqa_t = jnp.concatenate([nope, r1, r2, zpad], axis=-1).reshape(Q_LORA_RANK, hw).T.astype(BF16)
    wqb_t = jnp.concatenate([jnp.zeros_like(nope), r2, r1, zpad], axis=-1).reshape(Q_LORA_RANK, hw).T.astype(BF16)
    wkv = w_ukv.reshape(KV_LORA_RANK, MLA_HEADS, QK_NOPE_DIM + V_HEAD_DIM)
    wk = jnp.concatenate([wkv[..., :QK_NOPE_DIM], jnp.zeros((KV_LORA_RANK, MLA_HEADS, HEAD_PAD - QK_NOPE_DIM), F32)],
                         axis=-1).reshape(KV_LORA_RANK, hw).astype(BF16)
    wv_t = jnp.concatenate([wkv[..., QK_NOPE_DIM:], jnp.zeros((KV_LORA_RANK, MLA_HEADS, HEAD_PAD - V_HEAD_DIM), F32)],
                           axis=-1).reshape(KV_LORA_RANK, hw).T.astype(BF16)
    vones = jnp.zeros((MLA_HEADS, HEAD_PAD), F32).at[:, V_HEAD_DIM].set(1.0).reshape(hw, 1)
    wbm = jnp.concatenate(
        [w_br_mla.reshape(MLA_HEADS, V_HEAD_DIM, D_MODEL),
         jnp.zeros((MLA_HEADS, HEAD_PAD - V_HEAD_DIM, D_MODEL), F32)], axis=1).astype(BF16)
    n_r = N_GROUPS + N_EXPERTS
    wr = jnp.concatenate([w_rg, w_re, jnp.zeros((D_MODEL, LANES - n_r), F32)], axis=1)
    wr_hi = wr.astype(BF16)
    wr_lo = (wr - wr_hi.astype(F32)).astype(BF16)
    rw = jnp.concatenate([wr_hi, wr_lo], axis=1)
    rb = jnp.concatenate([b_rg, b_re, jnp.zeros((LANES - n_r,), F32)]).reshape(1, LANES)
    bs_full = jnp.repeat(b_s.T, SGU_GROUP_DIM, axis=1)
    fcol = (ROPE_THETA ** (-jnp.arange(ROPE_HALF, dtype=F32) / ROPE_HALF)).reshape(ROPE_HALF, 1)
    return w_in_p, wqa_t, wqb_t, wk, wv_t, vones, wbm, rw, rb, bs_full, fcol


def kernel(x, c, positions, w_ada, b_ada, norm1_g, w_in, q_norm_g, w_uq, kv_norm_g, w_ukv, v_norm_g, v_norm_b, w_s, b_s, w_br_mla, w_br_sgu, w_out, norm2_g, w_rg, b_rg, w_re, b_re, w1, w3, w2, final_g):
    batch, seq, _ = x.shape
    n_tok = batch * seq
    assert w_ada.shape[0] == 1, "single-layer block"
    l = 0
    row = lambda v: v.reshape(1, -1)
    (w_in_p, wqa_t, wqb_t, wk, wv_t, vones, wbm, rw, rb, bs_full, fcol) = _pack_weights(
        w_in[l], w_uq[l], w_ukv[l], w_br_mla[l], w_rg[l], b_rg[l], w_re[l], b_re[l], b_s[l])
    mod = _adaln(c, w_ada[l], b_ada[l]).reshape(batch, N_MOD, D_MODEL)
    qt, k, vt, gm, ms = _pre(
        x, mod, positions, row(norm1_g[l]), w_in_p, row(q_norm_g[l]), wqa_t, wqb_t, row(kv_norm_g[l]),
        wk, wv_t, vones, fcol, row(v_norm_g[l]), row(v_norm_b[l]), w_s[l], bs_full,
        w_br_sgu[l].astype(BF16))
    o = _flash(qt, k, vt)
    x2, h2, rt, rwt, cnt = _post(x, o, gm, ms, mod, wbm, w_out[l].astype(BF16), row(norm2_g[l]), rw, rb)

    counts = cnt[0, :N_EXPERTS].astype(jnp.int32)
    padded = ((counts + EXPERT_BLOCK - 1) // EXPERT_BLOCK) * EXPERT_BLOCK
    pad_ends = jnp.cumsum(padded)
    pad_starts = pad_ends - padded
    n_assign = n_tok * TOP_K
    rows_total = ((n_assign + EXPERT_BLOCK - 1) // EXPERT_BLOCK + N_EXPERTS) * EXPERT_BLOCK
    nblk = rows_total // EXPERT_BLOCK
    rti = rt.astype(jnp.int32)
    expert_ids = jnp.arange(N_EXPERTS, dtype=jnp.int32)
    dests = []
    for kk in range(TOP_K):
        e_k = rti[:, :, kk, :].reshape(n_tok)
        r_k = rti[:, :, TOP_K + kk, :].reshape(n_tok)
        start_k = jnp.sum(jnp.where(e_k[:, None] == expert_ids[None, :], pad_starts[None, :], 0), axis=1)
        dests.append(start_k + r_k)
    tok_ids = jnp.arange(n_tok, dtype=jnp.int32)
    slot_tok = (jnp.arange(rows_total, dtype=jnp.int32) % n_tok).at[jnp.concatenate(dests)].set(
        jnp.concatenate([tok_ids] * TOP_K))
    blk_start = jnp.arange(nblk, dtype=jnp.int32) * EXPERT_BLOCK
    blk_e = jnp.minimum(jnp.sum((pad_ends[None, :] <= blk_start[:, None]).astype(jnp.int32), axis=1),
                        N_EXPERTS - 1)
    n_real = pad_ends[-1:] // EXPERT_BLOCK
    x_sorted = h2.reshape(n_tok, D_MODEL)[slot_tok]
    y = _experts(blk_e, n_real, x_sorted, w1, w3, w2)
    y0, y1 = [y[d].reshape(batch, seq, D_MODEL) for d in dests]
    return _final(x2, y0, y1, rwt, mod, row(final_g))
```

```python
import jax
import jax.numpy as jnp
from jax import lax
from jax.experimental import pallas as pl
from jax.experimental.pallas import tpu as pltpu

F32 = jnp.float32
BF16 = jnp.bfloat16

D_MODEL = 1024
MLA_HEADS = 8
QK_NOPE_DIM = 64
QK_ROPE_DIM = 32
ROPE_HALF = QK_ROPE_DIM // 2
QK_DIM = QK_NOPE_DIM + QK_ROPE_DIM
V_HEAD_DIM = 64
Q_LORA_RANK = 256
KV_LORA_RANK = 128
ROPE_THETA = 10000.0
SGU_GROUPS = 8
SGU_GROUP_DIM = 64
SGU_WIDTH = SGU_GROUPS * SGU_GROUP_DIM
SGU_CHUNK = 128
N_GROUPS = 4
EXPERTS_PER_GROUP = 8
N_EXPERTS = N_GROUPS * EXPERTS_PER_GROUP
TOP_K = 2
EXPERT_FF = 512
N_MOD = 6
EPS = 1e-6

LANES = 128
HEAD_PAD = LANES
BF16_SUBLANES = 16
V_ROWS = -(-(V_HEAD_DIM + 1) // BF16_SUBLANES) * BF16_SUBLANES
VMEM_LIMIT = 48 * 1024 * 1024

TOKEN_TILE = 512
ATTN_TILE = 512
ATTN_CHAINS = 4
EXPERT_BLOCK = 256
ADALN_TILE = 512
ROUTE_ROWS = 8

_SEG_CQ = (0, 256)
_SEG_CKV = (256, 384)
_SEG_KR = (384, 512)
_SEG_U = (512, 1024)
_SEG_V = (1024, 1536)
_SEG_GMLA = (1536, 2560)
_SEG_GSGU = (2560, 3584)
PACKED_IN = 3584

Q_SCALE = (QK_DIM ** -0.5) * 1.4426950408889634

_NT_DIMS = (((1,), (1,)), ((), ()))


def _sigmoid(x):
    return 1.0 / (1.0 + jnp.exp(-x))


def _gelu_tanh(x):
    c = 0.7978845608028654
    return 0.5 * x * (1.0 + jnp.tanh(c * (x + 0.044715 * (x * x * x))))


def _rms(x, g):
    return x * lax.rsqrt(jnp.mean(x * x, axis=-1, keepdims=True) + EPS) * g


def _dot(a, b):
    return jnp.dot(a, b, preferred_element_type=F32)


def _dot_nt(a, b):
    return lax.dot_general(a, b, _NT_DIMS, preferred_element_type=F32)


def _adaln_kernel(ct_ref, w_ref, b_ref, o_ref):
    ct = ct_ref[...]
    s = ct * _sigmoid(ct)
    w = w_ref[...]
    rows = [jnp.sum(s[:, b:b + 1] * w, axis=0, keepdims=True) for b in range(ct.shape[1])]
    o_ref[...] = jnp.concatenate(rows, axis=0) + b_ref[...]


def _adaln(c, w_ada, b_ada):
    batch = c.shape[0]
    width = w_ada.shape[1]
    return pl.pallas_call(
        _adaln_kernel,
        grid=(width // ADALN_TILE,),
        in_specs=[
            pl.BlockSpec((D_MODEL, batch), lambda j: (0, 0)),
            pl.BlockSpec((D_MODEL, ADALN_TILE), lambda j: (0, j)),
            pl.BlockSpec((1, ADALN_TILE), lambda j: (0, j)),
        ],
        out_specs=pl.BlockSpec((batch, ADALN_TILE), lambda j: (0, j)),
        out_shape=jax.ShapeDtypeStruct((batch, width), F32),
        compiler_params=pltpu.CompilerParams(dimension_semantics=("arbitrary",)),
        name="adaln",
    )(c.T, w_ada, b_ada.reshape(1, width))


def _pre_kernel(x_ref, mod_ref, pos_ref, g1_ref, win_ref, qg_ref, wqa_ref, wqb_ref,
                kvg_ref, wk_ref, wv_ref, vones_ref, fcol_ref,
                vng_ref, vnb_ref, ws_ref, bs_ref, wbs_ref,
                qt_out, k_out, vt_out, gm_out, ms_out):
    tm = x_ref.shape[1]
    x = x_ref[0]
    shift1 = mod_ref[0, 0:1, :]
    scale1 = mod_ref[0, 1:2, :]
    h1 = _rms(x, g1_ref[...]) * (1.0 + scale1) + shift1
    hb = h1.astype(BF16)

    def proj(seg):
        return _dot(hb, win_ref[:, seg[0]:seg[1]])

    pos = pos_ref[0].astype(F32)
    ang = fcol_ref[...] * pos
    cos_t = jnp.cos(ang)
    sin_t = jnp.sin(ang)
    pad_rows = HEAD_PAD - QK_DIM

    cqn = _rms(proj(_SEG_CQ), qg_ref[...]).astype(BF16)
    qa_t = _dot_nt(wqa_ref[...], cqn)
    qb_t = _dot_nt(wqb_ref[...], cqn)
    c_tab = jnp.concatenate([jnp.ones((QK_NOPE_DIM, tm), F32), cos_t, cos_t,
                             jnp.ones((pad_rows, tm), F32)], axis=0) * Q_SCALE
    s_tab = jnp.concatenate([jnp.zeros((QK_NOPE_DIM, tm), F32), -sin_t, sin_t,
                             jnp.zeros((pad_rows, tm), F32)], axis=0) * Q_SCALE
    for h in range(MLA_HEADS):
        sl = slice(h * HEAD_PAD, (h + 1) * HEAD_PAD)
        qt_out[0, h, 0] = (qa_t[sl] * c_tab + qb_t[sl] * s_tab).astype(BF16)

    ckvn = _rms(proj(_SEG_CKV), kvg_ref[...]).astype(BF16)
    kr_t = proj(_SEG_KR).T
    x1 = kr_t[0:ROPE_HALF]
    x2 = kr_t[ROPE_HALF:QK_ROPE_DIM]
    k_pe = jnp.concatenate([jnp.zeros((QK_NOPE_DIM, tm), F32), x1 * cos_t - x2 * sin_t,
                            x1 * sin_t + x2 * cos_t, jnp.zeros((pad_rows, tm), F32)], axis=0).T
    k_full = _dot(ckvn, wk_ref[...])
    v_t = _dot_nt(wv_ref[...], ckvn) + vones_ref[...]
    for h in range(MLA_HEADS):
        sl = slice(h * HEAD_PAD, (h + 1) * HEAD_PAD)
        k_out[0, h] = (k_full[:, sl] + k_pe).astype(BF16)
        vt_out[0, h, 0] = v_t[h * V_ROWS:(h + 1) * V_ROWS].astype(BF16)

    gu = _gelu_tanh(proj(_SEG_U))
    gv = _gelu_tanh(proj(_SEG_V))
    mu = jnp.mean(gv, axis=-1, keepdims=True)
    dv = gv - mu
    var = jnp.mean(dv * dv, axis=-1, keepdims=True)
    vln = (dv * lax.rsqrt(var + EPS) * vng_ref[...] + vnb_ref[...]).astype(BF16)
    row = lax.broadcasted_iota(jnp.int32, (SGU_CHUNK, SGU_CHUNK), 0)
    col = lax.broadcasted_iota(jnp.int32, (SGU_CHUNK, SGU_CHUNK), 1)
    causal = col <= row
    w_tril = [jnp.where(causal, ws_ref[g], 0.0).astype(BF16) for g in range(SGU_GROUPS)]
    lane_grp = lax.broadcasted_iota(jnp.int32, (SGU_CHUNK, SGU_WIDTH), 1) // SGU_GROUP_DIM
    chunks = []
    for c in range(tm // SGU_CHUNK):
        vc = vln[c * SGU_CHUNK:(c + 1) * SGU_CHUNK]
        acc = jnp.zeros((SGU_CHUNK, SGU_WIDTH), F32)
        for g in range(SGU_GROUPS):
            acc = jnp.where(lane_grp == g, _dot(w_tril[g], vc), acc)
        chunks.append(acc + bs_ref[...])
    sgu = (gu * jnp.concatenate(chunks, axis=0)).astype(BF16)
    y_sgu = _dot(sgu, wbs_ref[...])
    ms_out[0] = (_sigmoid(proj(_SEG_GSGU)) * y_sgu).astype(BF16)
    gm_out[0] = _sigmoid(proj(_SEG_GMLA)).astype(BF16)


def _pre(x, mod, positions, g1, w_in_p, qg, wqa_t, wqb_t, kvg, wk, wv_t, vones, fcol,
         vng, vnb, w_s, bs_full, wbs):
    batch, seq, _ = x.shape
    tm = TOKEN_TILE
    assert tm == ATTN_TILE and seq % tm == 0
    nt = seq // tm
    hw = MLA_HEADS * HEAD_PAD

    def const(shape):
        return pl.BlockSpec(shape, lambda b, i: (0,) * len(shape))

    return pl.pallas_call(
        _pre_kernel,
        grid=(batch, nt),
        in_specs=[
            pl.BlockSpec((1, tm, D_MODEL), lambda b, i: (b, i, 0)),
            pl.BlockSpec((1, N_MOD, D_MODEL), lambda b, i: (b, 0, 0)),
            pl.BlockSpec((1, 1, tm), lambda b, i: (b, 0, i)),
            const((1, D_MODEL)),
            const((D_MODEL, PACKED_IN)),
            const((1, Q_LORA_RANK)),
            const((hw, Q_LORA_RANK)),
            const((hw, Q_LORA_RANK)),
            const((1, KV_LORA_RANK)),
            const((KV_LORA_RANK, hw)),
            const((MLA_HEADS * V_ROWS, KV_LORA_RANK)),
            const((MLA_HEADS * V_ROWS, 1)),
            const((ROPE_HALF, 1)),
            const((1, SGU_WIDTH)),
            const((1, SGU_WIDTH)),
            const((SGU_GROUPS, SGU_CHUNK, SGU_CHUNK)),
            const((SGU_CHUNK, SGU_WIDTH)),
            const((SGU_WIDTH, D_MODEL)),
        ],
        out_specs=[
            pl.BlockSpec((1, MLA_HEADS, 1, HEAD_PAD, tm), lambda b, i: (b, 0, i, 0, 0)),
            pl.BlockSpec((1, MLA_HEADS, tm, HEAD_PAD), lambda b, i: (b, 0, i, 0)),
            pl.BlockSpec((1, MLA_HEADS, 1, V_ROWS, tm), lambda b, i: (b, 0, i, 0, 0)),
            pl.BlockSpec((1, tm, D_MODEL), lambda b, i: (b, i, 0)),
            pl.BlockSpec((1, tm, D_MODEL), lambda b, i: (b, i, 0)),
        ],
        out_shape=[
            jax.ShapeDtypeStruct((batch, MLA_HEADS, nt, HEAD_PAD, tm), BF16),
            jax.ShapeDtypeStruct((batch, MLA_HEADS, seq, HEAD_PAD), BF16),
            jax.ShapeDtypeStruct((batch, MLA_HEADS, nt, V_ROWS, tm), BF16),
            jax.ShapeDtypeStruct((batch, seq, D_MODEL), BF16),
            jax.ShapeDtypeStruct((batch, seq, D_MODEL), BF16),
        ],
        compiler_params=pltpu.CompilerParams(
            dimension_semantics=("arbitrary", "arbitrary"), vmem_limit_bytes=VMEM_LIMIT),
        name="pre",
    )(x, mod, positions.reshape(batch, 1, seq), g1, w_in_p, qg, wqa_t, wqb_t, kvg, wk, wv_t, vones,
      fcol, vng, vnb, w_s, bs_full, wbs)


def _flash_kernel(qt_ref, k_ref, vt_ref, o_ref, m_ref, acc_ref, s0_ref, s1_ref):
    t = qt_ref.shape[4]
    i = pl.program_id(2)
    m_ref[...] = jnp.full(m_ref.shape, -1e30, F32)
    acc_ref[...] = jnp.zeros(acc_ref.shape, F32)

    def logits(c, j):
        start = pl.multiple_of(j * t, t)
        return _dot(k_ref[0, 0, pl.ds(start, t), :], qt_ref[0, 0, c])

    def update(c, j, s_ref, masked):
        def read():
            s = s_ref[c]
            if masked:
                key = lax.broadcasted_iota(jnp.int32, (t, t), 0)
                qry = lax.broadcasted_iota(jnp.int32, (t, t), 1)
                s = jnp.where(key <= qry, s, -jnp.inf)
            return s
        m_old = m_ref[c]
        m_new = jnp.maximum(m_old, jnp.max(read(), axis=0, keepdims=True))
        p = jnp.exp2(read() - m_new).astype(BF16)
        pv = _dot(vt_ref[0, 0, j], p)
        acc_ref[c] = jnp.exp2(m_old - m_new) * acc_ref[c] + pv
        m_ref[c] = m_new

    bufs = (s0_ref, s1_ref)
    chains = list(range(ATTN_CHAINS))
    last = ATTN_CHAINS - 1
    first = ATTN_CHAINS * i

    for c in chains:
        s0_ref[c] = logits(c, 0)
    for c in chains[:last]:
        s1_ref[c] = logits(c, 1)

    def body(u, carry):
        base = ATTN_CHAINS * u
        s1_ref[last] = logits(last, base + 1)
        for kk in range(ATTN_CHAINS):
            for c in chains:
                update(c, base + kk, bufs[kk % 2], False)
                if not (kk == ATTN_CHAINS - 1 and c == last):
                    bufs[kk % 2][c] = logits(c, base + kk + 2)
        return carry

    lax.fori_loop(0, i, body, 0)
    s1_ref[last] = logits(last, first + 1)
    for d in range(ATTN_CHAINS):
        for c in chains[d:]:
            update(c, first + d, bufs[d % 2], c == d)
            if c >= d + 2:
                bufs[d % 2][c] = logits(c, first + d + 2)
    for c in range(ATTN_CHAINS):
        acc = acc_ref[c]
        o_t = jnp.concatenate([acc / acc[V_HEAD_DIM:V_HEAD_DIM + 1, :],
                               jnp.zeros((HEAD_PAD - V_ROWS, t), F32)], axis=0)
        o_ref[0, 0, c * t:(c + 1) * t, :] = o_t.T.astype(BF16)


def _flash(qt, k, vt):
    batch, heads, nt, _, t = qt.shape
    seq = nt * t
    assert nt % ATTN_CHAINS == 0 and ATTN_CHAINS % 2 == 0
    return pl.pallas_call(
        _flash_kernel,
        grid=(batch, heads, nt // ATTN_CHAINS),
        in_specs=[
            pl.BlockSpec((1, 1, ATTN_CHAINS, HEAD_PAD, t), lambda b, h, i: (b, h, i, 0, 0)),
            pl.BlockSpec((1, 1, seq, HEAD_PAD), lambda b, h, i: (b, h, 0, 0)),
            pl.BlockSpec((1, 1, nt, V_ROWS, t), lambda b, h, i: (b, h, 0, 0, 0)),
        ],
        out_specs=pl.BlockSpec((1, 1, ATTN_CHAINS * t, HEAD_PAD), lambda b, h, i: (b, h, i, 0)),
        out_shape=jax.ShapeDtypeStruct((batch, heads, seq, HEAD_PAD), BF16),
        scratch_shapes=[pltpu.VMEM((ATTN_CHAINS, 1, t), F32), pltpu.VMEM((ATTN_CHAINS, V_ROWS, t), F32),
                        pltpu.VMEM((ATTN_CHAINS, t, t), F32), pltpu.VMEM((ATTN_CHAINS, t, t), F32)],
        compiler_params=pltpu.CompilerParams(
            dimension_semantics=("arbitrary", "arbitrary", "arbitrary"),
            vmem_limit_bytes=VMEM_LIMIT),
        name="flash",
    )(qt, k, vt)


def _post_kernel(x_ref, o_ref, gm_ref, ms_ref, mod_ref, wbm_ref, wout_ref, g2_ref, rw_ref, rb_ref,
                 x2_out, h2_out, rt_out, rwt_out, cnt_out, carry_ref):
    tm = x_ref.shape[1]

    @pl.when((pl.program_id(0) == 0) & (pl.program_id(1) == 0))
    def _():
        carry_ref[...] = jnp.zeros(carry_ref.shape, F32)

    gate1 = mod_ref[0, 2:3, :]
    shift2 = mod_ref[0, 3:4, :]
    scale2 = mod_ref[0, 4:5, :]
    y_mla = _dot(o_ref[0, 0], wbm_ref[0])
    for h in range(1, MLA_HEADS):
        y_mla = y_mla + _dot(o_ref[0, h], wbm_ref[h])
    merged = gm_ref[0].astype(F32) * y_mla + ms_ref[0].astype(F32)
    x2 = x_ref[0] + gate1 * _dot(merged.astype(BF16), wout_ref[...])
    x2_out[0] = x2
    h2 = _rms(x2, g2_ref[...]) * (1.0 + scale2) + shift2
    h2_out[0] = h2.astype(BF16)

    hi = h2.astype(BF16)
    lo = (h2 - hi.astype(F32)).astype(BF16)
    a = _dot(hi, rw_ref[...])
    b = _dot(lo, rw_ref[:, 0:LANES])
    logits = a[:, 0:LANES] + a[:, LANES:2 * LANES] + b + rb_ref[...]

    lane = lax.broadcasted_iota(jnp.int32, (tm, LANES), 1).astype(F32)
    big = float(LANES)
    ninf = -jnp.inf
    gl = jnp.where(lane < N_GROUPS, logits, ninf)
    gmax = jnp.max(gl, axis=-1, keepdims=True)
    grp = jnp.min(jnp.where(gl == gmax, lane, big), axis=-1, keepdims=True)
    p_grp = 1.0 / jnp.sum(jnp.exp(gl - gmax), axis=-1, keepdims=True)
    first = N_GROUPS + EXPERTS_PER_GROUP * grp
    el = jnp.where((lane >= first) & (lane < first + EXPERTS_PER_GROUP), logits, ninf)
    m1 = jnp.max(el, axis=-1, keepdims=True)
    i1 = jnp.min(jnp.where(el == m1, lane, big), axis=-1, keepdims=True)
    el2 = jnp.where(lane == i1, ninf, el)
    m2 = jnp.max(el2, axis=-1, keepdims=True)
    i2 = jnp.min(jnp.where(el2 == m2, lane, big), axis=-1, keepdims=True)
    tt = jnp.exp(m2 - m1)
    w0 = p_grp / (1.0 + tt)
    w1 = p_grp * tt / (1.0 + tt)
    e0 = i1 - N_GROUPS
    e1 = i2 - N_GROUPS

    onehot = jnp.where((lane == e0) | (lane == e1), 1.0, 0.0)
    row = lax.broadcasted_iota(jnp.int32, (tm, tm), 0)
    col = lax.broadcasted_iota(jnp.int32, (tm, tm), 1)
    lower = jnp.where(col < row, 1.0, 0.0).astype(BF16)
    before = _dot(lower, onehot.astype(BF16)) + carry_ref[...]
    r0 = jnp.sum(jnp.where(lane == e0, before, 0.0), axis=-1, keepdims=True)
    r1 = jnp.sum(jnp.where(lane == e1, before, 0.0), axis=-1, keepdims=True)
    carry_ref[...] = carry_ref[...] + jnp.sum(onehot, axis=0, keepdims=True)
    cnt_out[...] = carry_ref[...]

    ri = jnp.where(lane == 0, e0, jnp.where(lane == 1, e1, jnp.where(lane == 2, r0, jnp.where(lane == 3, r1, 0.0))))
    rt_out[0, 0] = ri.T[0:ROUTE_ROWS]
    rwt_out[0] = jnp.where(lane == 0, w0, jnp.where(lane == 1, w1, 0.0))


def _post(x, o, gm, ms, mod, wbm, wout, g2, rw, rb):
    batch, seq, _ = x.shape
    tm = TOKEN_TILE

    def const(shape):
        return pl.BlockSpec(shape, lambda b, i: (0,) * len(shape))

    tok = lambda w: pl.BlockSpec((1, tm, w), lambda b, i: (b, i, 0))
    return pl.pallas_call(
        _post_kernel,
        grid=(batch, seq // tm),
        in_specs=[
            tok(D_MODEL),
            pl.BlockSpec((1, MLA_HEADS, tm, HEAD_PAD), lambda b, i: (b, 0, i, 0)),
            tok(D_MODEL),
            tok(D_MODEL),
            pl.BlockSpec((1, N_MOD, D_MODEL), lambda b, i: (b, 0, 0)),
            const((MLA_HEADS, HEAD_PAD, D_MODEL)),
            const((D_MODEL, D_MODEL)),
            const((1, D_MODEL)),
            const((D_MODEL, 2 * LANES)),
            const((1, LANES)),
        ],
        out_specs=[tok(D_MODEL), tok(D_MODEL),
                   pl.BlockSpec((1, 1, ROUTE_ROWS, tm), lambda b, i: (b, i, 0, 0)),
                   tok(LANES), const((1, LANES))],
        out_shape=[
            jax.ShapeDtypeStruct((batch, seq, D_MODEL), F32),
            jax.ShapeDtypeStruct((batch, seq, D_MODEL), BF16),
            jax.ShapeDtypeStruct((batch, seq // tm, ROUTE_ROWS, tm), F32),
            jax.ShapeDtypeStruct((batch, seq, LANES), F32),
            jax.ShapeDtypeStruct((1, LANES), F32),
        ],
        scratch_shapes=[pltpu.VMEM((1, LANES), F32)],
        compiler_params=pltpu.CompilerParams(
            dimension_semantics=("arbitrary", "arbitrary"), vmem_limit_bytes=VMEM_LIMIT),
        name="post",
    )(x, o, gm, ms, mod, wbm, wout, g2, rw, rb)


def _expert_kernel(be_ref, nr_ref, x_ref, w1_ref, w3_ref, w2_ref, y_ref):
    j = pl.program_id(0)

    @pl.when(j < nr_ref[0])
    def _():
        x = x_ref[...]
        a = _dot(x, w1_ref[0, 0].astype(BF16))
        b = _dot(x, w3_ref[0, 0].astype(BF16))
        act = (a * _sigmoid(a) * b).astype(BF16)
        y_ref[...] = _dot(act, w2_ref[0, 0].astype(BF16)).astype(BF16)

    @pl.when(j >= nr_ref[0])
    def _():
        y_ref[...] = jnp.zeros(y_ref.shape, BF16)


def _experts(blk_e, n_real, x_sorted, w1, w3, w2):
    rows = x_sorted.shape[0]
    nblk = rows // EXPERT_BLOCK
    last = lambda j, be, nr: jnp.minimum(j, nr[0] - 1)
    w_map = lambda j, be, nr: (0, be[last(j, be, nr)], 0, 0)
    return pl.pallas_call(
        _expert_kernel,
        grid_spec=pltpu.PrefetchScalarGridSpec(
            num_scalar_prefetch=2,
            grid=(nblk,),
            in_specs=[
                pl.BlockSpec((EXPERT_BLOCK, D_MODEL), lambda j, be, nr: (last(j, be, nr), 0)),
                pl.BlockSpec((1, 1, D_MODEL, EXPERT_FF), w_map),
                pl.BlockSpec((1, 1, D_MODEL, EXPERT_FF), w_map),
                pl.BlockSpec((1, 1, EXPERT_FF, D_MODEL), w_map),
            ],
            out_specs=pl.BlockSpec((EXPERT_BLOCK, D_MODEL), lambda j, be, nr: (j, 0)),
        ),
        out_shape=jax.ShapeDtypeStruct((rows, D_MODEL), BF16),
        compiler_params=pltpu.CompilerParams(
            dimension_semantics=("arbitrary",), vmem_limit_bytes=VMEM_LIMIT),
        name="experts",
    )(blk_e, n_real, x_sorted, w1, w3, w2)


def _final_kernel(x2_ref, y0_ref, y1_ref, rwt_ref, mod_ref, fg_ref, out_ref):
    gate2 = mod_ref[0, 5:6, :]
    rwt = rwt_ref[0]
    moe = rwt[:, 0:1] * y0_ref[0].astype(F32) + rwt[:, 1:2] * y1_ref[0].astype(F32)
    x3 = x2_ref[0] + gate2 * moe
    out_ref[0] = _rms(x3, fg_ref[...])


def _final(x2, y0, y1, rwt, mod, fg):
    batch, seq, _ = x2.shape
    tm = TOKEN_TILE
    tok = lambda w: pl.BlockSpec((1, tm, w), lambda b, i: (b, i, 0))
    return pl.pallas_call(
        _final_kernel,
        grid=(batch, seq // tm),
        in_specs=[
            tok(D_MODEL), tok(D_MODEL), tok(D_MODEL), tok(LANES),
            pl.BlockSpec((1, N_MOD, D_MODEL), lambda b, i: (b, 0, 0)),
            pl.BlockSpec((1, D_MODEL), lambda b, i: (0, 0)),
        ],
        out_specs=tok(D_MODEL),
        out_shape=jax.ShapeDtypeStruct((batch, seq, D_MODEL), F32),
        compiler_params=pltpu.CompilerParams(
            dimension_semantics=("arbitrary", "arbitrary"), vmem_limit_bytes=VMEM_LIMIT),
        name="final",
    )(x2, y0, y1, rwt, mod, fg)


def _pack_weights(w_in, w_uq, w_ukv, w_br_mla, w_rg, b_rg, w_re, b_re, b_s):
    kr_end = Q_LORA_RANK + KV_LORA_RANK + QK_ROPE_DIM
    w_in_p = jnp.concatenate(
        [w_in[:, :kr_end], jnp.zeros((D_MODEL, _SEG_KR[1] - kr_end), F32), w_in[:, kr_end:]],
        axis=1).astype(BF16)
    pad = HEAD_PAD - QK_DIM
    hw = MLA_HEADS * HEAD_PAD
    wq = w_uq.reshape(Q_LORA_RANK, MLA_HEADS, QK_DIM)
    nope, r1, r2 = wq[..., :QK_NOPE_DIM], wq[..., QK_NOPE_DIM:QK_NOPE_DIM + ROPE_HALF], wq[..., QK_NOPE_DIM + ROPE_HALF:]
    zpad = jnp.zeros((Q_LORA_RANK, MLA_HEADS, pad), F32)
    wqa_t = jnp.concatenate([nope, r1, r2, zpad], axis=-1).reshape(Q_LORA_RANK, hw).T.astype(BF16)
    wqb_t = jnp.concatenate([jnp.zeros_like(nope), r2, r1, zpad], axis=-1).reshape(Q_LORA_RANK, hw).T.astype(BF16)
    wkv = w_ukv.reshape(KV_LORA_RANK, MLA_HEADS, QK_NOPE_DIM + V_HEAD_DIM)
    wk = jnp.concatenate([wkv[..., :QK_NOPE_DIM], jnp.zeros((KV_LORA_RANK, MLA_HEADS, HEAD_PAD - QK_NOPE_DIM), F32)],
                         axis=-1).reshape(KV_LORA_RANK, hw).astype(BF16)
    wv_t = jnp.concatenate([wkv[..., QK_NOPE_DIM:], jnp.zeros((KV_LORA_RANK, MLA_HEADS, V_ROWS - V_HEAD_DIM), F32)],
                           axis=-1).reshape(KV_LORA_RANK, MLA_HEADS * V_ROWS).T.astype(BF16)
    vones = jnp.zeros((MLA_HEADS, V_ROWS), F32).at[:, V_HEAD_DIM].set(1.0).reshape(MLA_HEADS * V_ROWS, 1)
    wbm = jnp.concatenate(
        [w_br_mla.reshape(MLA_HEADS, V_HEAD_DIM, D_MODEL),
         jnp.zeros((MLA_HEADS, HEAD_PAD - V_HEAD_DIM, D_MODEL), F32)], axis=1).astype(BF16)
    n_r = N_GROUPS + N_EXPERTS
    wr = jnp.concatenate([w_rg, w_re, jnp.zeros((D_MODEL, LANES - n_r), F32)], axis=1)
    wr_hi = wr.astype(BF16)
    wr_lo = (wr - wr_hi.astype(F32)).astype(BF16)
    rw = jnp.concatenate([wr_hi, wr_lo], axis=1)
    rb = jnp.concatenate([b_rg, b_re, jnp.zeros((LANES - n_r,), F32)]).reshape(1, LANES)
    bs_full = jnp.repeat(b_s.T, SGU_GROUP_DIM, axis=1)
    fcol = (ROPE_THETA ** (-jnp.arange(ROPE_HALF, dtype=F32) / ROPE_HALF)).reshape(ROPE_HALF, 1)
    return w_in_p, wqa_t, wqb_t, wk, wv_t, vones, wbm, rw, rb, bs_full, fcol


def kernel(x, c, positions, w_ada, b_ada, norm1_g, w_in, q_norm_g, w_uq, kv_norm_g, w_ukv, v_norm_g, v_norm_b, w_s, b_s, w_br_mla, w_br_sgu, w_out, norm2_g, w_rg, b_rg, w_re, b_re, w1, w3, w2, final_g):
    batch, seq, _ = x.shape
    n_tok = batch * seq
    assert w_ada.shape[0] == 1, "single-layer block"
    l = 0
    row = lambda v: v.reshape(1, -1)
    (w_in_p, wqa_t, wqb_t, wk, wv_t, vones, wbm, rw, rb, bs_full, fcol) = _pack_weights(
        w_in[l], w_uq[l], w_ukv[l], w_br_mla[l], w_rg[l], b_rg[l], w_re[l], b_re[l], b_s[l])
    mod = _adaln(c, w_ada[l], b_ada[l]).reshape(batch, N_MOD, D_MODEL)
    qt, k, vt, gm, ms = _pre(
        x, mod, positions, row(norm1_g[l]), w_in_p, row(q_norm_g[l]), wqa_t, wqb_t, row(kv_norm_g[l]),
        wk, wv_t, vones, fcol, row(v_norm_g[l]), row(v_norm_b[l]), w_s[l], bs_full,
        w_br_sgu[l].astype(BF16))
    o = _flash(qt, k, vt)
    x2, h2, rt, rwt, cnt = _post(x, o, gm, ms, mod, wbm, w_out[l].astype(BF16), row(norm2_g[l]), rw, rb)

    counts = cnt[0, :N_EXPERTS].astype(jnp.int32)
    padded = ((counts + EXPERT_BLOCK - 1) // EXPERT_BLOCK) * EXPERT_BLOCK
    pad_ends = jnp.cumsum(padded)
    pad_starts = pad_ends - padded
    n_assign = n_tok * TOP_K
    rows_total = ((n_assign + EXPERT_BLOCK - 1) // EXPERT_BLOCK + N_EXPERTS) * EXPERT_BLOCK
    nblk = rows_total // EXPERT_BLOCK
    rti = rt.astype(jnp.int32)
    expert_ids = jnp.arange(N_EXPERTS, dtype=jnp.int32)
    dests = []
    for kk in range(TOP_K):
        e_k = rti[:, :, kk, :].reshape(n_tok)
        r_k = rti[:, :, TOP_K + kk, :].reshape(n_tok)
        start_k = jnp.sum(jnp.where(e_k[:, None] == expert_ids[None, :], pad_starts[None, :], 0), axis=1)
        dests.append(start_k + r_k)
    tok_ids = jnp.arange(n_tok, dtype=jnp.int32)
    slot_tok = (jnp.arange(rows_total, dtype=jnp.int32) % n_tok).at[jnp.concatenate(dests)].set(
        jnp.concatenate([tok_ids] * TOP_K), unique_indices=True, mode='promise_in_bounds')
    blk_start = jnp.arange(nblk, dtype=jnp.int32) * EXPERT_BLOCK
    blk_e = jnp.minimum(jnp.sum((pad_ends[None, :] <= blk_start[:, None]).astype(jnp.int32), axis=1),
                        N_EXPERTS - 1)
    n_real = pad_ends[-1:] // EXPERT_BLOCK
    x_sorted = h2.reshape(n_tok, D_MODEL)[slot_tok]
    y = _experts(blk_e, n_real, x_sorted, w1, w3, w2)
    y0, y1 = [y[d].reshape(batch, seq, D_MODEL) for d in dests]
    return _final(x2, y0, y1, rwt, mod, row(final_g))
```

```python
import jax
import jax.numpy as jnp
from jax import lax
from jax.experimental import pallas as pl
from jax.experimental.pallas import tpu as pltpu

F32 = jnp.float32
BF16 = jnp.bfloat16

D_MODEL = 1024
MLA_HEADS = 8
QK_NOPE_DIM = 64
QK_ROPE_DIM = 32
ROPE_HALF = QK_ROPE_DIM // 2
QK_DIM = QK_NOPE_DIM + QK_ROPE_DIM
V_HEAD_DIM = 64
Q_LORA_RANK = 256
KV_LORA_RANK = 128
ROPE_THETA = 10000.0
SGU_GROUPS = 8
SGU_GROUP_DIM = 64
SGU_WIDTH = SGU_GROUPS * SGU_GROUP_DIM
SGU_CHUNK = 128
N_GROUPS = 4
EXPERTS_PER_GROUP = 8
N_EXPERTS = N_GROUPS * EXPERTS_PER_GROUP
TOP_K = 2
EXPERT_FF = 512
N_MOD = 6
EPS = 1e-6

LANES = 128
HEAD_PAD = LANES
BF16_SUBLANES = 16
V_ROWS = -(-(V_HEAD_DIM + 1) // BF16_SUBLANES) * BF16_SUBLANES
VMEM_LIMIT = 48 * 1024 * 1024

TOKEN_TILE = 512
ATTN_TILE = 512
ATTN_CHAINS = 4
EXPERT_BLOCK = 512
ADALN_TILE = 512
ROUTE_ROWS = 8
SLOT_MAP_UNROLL = 8

_SEG_CQ = (0, 256)
_SEG_CKV = (256, 384)
_SEG_KR = (384, 512)
_SEG_U = (512, 1024)
_SEG_V = (1024, 1536)
_SEG_GMLA = (1536, 2560)
_SEG_GSGU = (2560, 3584)
PACKED_IN = 3584

Q_SCALE = (QK_DIM ** -0.5) * 1.4426950408889634

_NT_DIMS = (((1,), (1,)), ((), ()))


def _sigmoid(x):
    return 1.0 / (1.0 + jnp.exp(-x))


def _gelu_tanh(x):
    c = 0.7978845608028654
    return 0.5 * x * (1.0 + jnp.tanh(c * (x + 0.044715 * (x * x * x))))


def _rms(x, g):
    return x * lax.rsqrt(jnp.mean(x * x, axis=-1, keepdims=True) + EPS) * g


def _dot(a, b):
    return jnp.dot(a, b, preferred_element_type=F32)


def _dot_nt(a, b):
    return lax.dot_general(a, b, _NT_DIMS, preferred_element_type=F32)


def _adaln_kernel(ct_ref, w_ref, b_ref, o_ref):
    ct = ct_ref[...]
    s = ct * _sigmoid(ct)
    w = w_ref[...]
    rows = [jnp.sum(s[:, b:b + 1] * w, axis=0, keepdims=True) for b in range(ct.shape[1])]
    o_ref[...] = jnp.concatenate(rows, axis=0) + b_ref[...]


def _adaln(c, w_ada, b_ada):
    batch = c.shape[0]
    width = w_ada.shape[1]
    return pl.pallas_call(
        _adaln_kernel,
        grid=(width // ADALN_TILE,),
        in_specs=[
            pl.BlockSpec((D_MODEL, batch), lambda j: (0, 0)),
            pl.BlockSpec((D_MODEL, ADALN_TILE), lambda j: (0, j)),
            pl.BlockSpec((1, ADALN_TILE), lambda j: (0, j)),
        ],
        out_specs=pl.BlockSpec((batch, ADALN_TILE), lambda j: (0, j)),
        out_shape=jax.ShapeDtypeStruct((batch, width), F32),
        compiler_params=pltpu.CompilerParams(dimension_semantics=("arbitrary",)),
        name="adaln",
    )(c.T, w_ada, b_ada.reshape(1, width))


def _pre_kernel(x_ref, mod_ref, pos_ref, g1_ref, win_ref, qg_ref, wqa_ref, wqb_ref,
                kvg_ref, wk_ref, wv_ref, vones_ref, fcol_ref,
                vng_ref, vnb_ref, ws_ref, bs_ref, wbs_ref,
                qt_out, k_out, vt_out, gm_out, ms_out):
    tm = x_ref.shape[1]
    x = x_ref[0]
    shift1 = mod_ref[0, 0:1, :]
    scale1 = mod_ref[0, 1:2, :]
    h1 = _rms(x, g1_ref[...]) * (1.0 + scale1) + shift1
    hb = h1.astype(BF16)

    def proj(seg):
        return _dot(hb, win_ref[:, seg[0]:seg[1]])

    pos = pos_ref[0].astype(F32)
    ang = fcol_ref[...] * pos
    cos_t = jnp.cos(ang)
    sin_t = jnp.sin(ang)
    pad_rows = HEAD_PAD - QK_DIM

    cqn = _rms(proj(_SEG_CQ), qg_ref[...]).astype(BF16)
    qa_t = _dot_nt(wqa_ref[...], cqn)
    qb_t = _dot_nt(wqb_ref[...], cqn)
    c_tab = jnp.concatenate([jnp.ones((QK_NOPE_DIM, tm), F32), cos_t, cos_t,
                             jnp.ones((pad_rows, tm), F32)], axis=0) * Q_SCALE
    s_tab = jnp.concatenate([jnp.zeros((QK_NOPE_DIM, tm), F32), -sin_t, sin_t,
                             jnp.zeros((pad_rows, tm), F32)], axis=0) * Q_SCALE
    for h in range(MLA_HEADS):
        sl = slice(h * HEAD_PAD, (h + 1) * HEAD_PAD)
        qt_out[0, h, 0] = (qa_t[sl] * c_tab + qb_t[sl] * s_tab).astype(BF16)

    ckvn = _rms(proj(_SEG_CKV), kvg_ref[...]).astype(BF16)
    kr_t = proj(_SEG_KR).T
    x1 = kr_t[0:ROPE_HALF]
    x2 = kr_t[ROPE_HALF:QK_ROPE_DIM]
    k_pe = jnp.concatenate([jnp.zeros((QK_NOPE_DIM, tm), F32), x1 * cos_t - x2 * sin_t,
                            x1 * sin_t + x2 * cos_t, jnp.zeros((pad_rows, tm), F32)], axis=0).T
    k_full = _dot(ckvn, wk_ref[...])
    v_t = _dot_nt(wv_ref[...], ckvn) + vones_ref[...]
    for h in range(MLA_HEADS):
        sl = slice(h * HEAD_PAD, (h + 1) * HEAD_PAD)
        k_out[0, h] = (k_full[:, sl] + k_pe).astype(BF16)
        vt_out[0, h, 0] = v_t[h * V_ROWS:(h + 1) * V_ROWS].astype(BF16)

    gu = _gelu_tanh(proj(_SEG_U))
    gv = _gelu_tanh(proj(_SEG_V))
    mu = jnp.mean(gv, axis=-1, keepdims=True)
    dv = gv - mu
    var = jnp.mean(dv * dv, axis=-1, keepdims=True)
    vln = (dv * lax.rsqrt(var + EPS) * vng_ref[...] + vnb_ref[...]).astype(BF16)
    row = lax.broadcasted_iota(jnp.int32, (SGU_CHUNK, SGU_CHUNK), 0)
    col = lax.broadcasted_iota(jnp.int32, (SGU_CHUNK, SGU_CHUNK), 1)
    causal = col <= row
    w_tril = [jnp.where(causal, ws_ref[g], 0.0).astype(BF16) for g in range(SGU_GROUPS)]
    lane_grp = lax.broadcasted_iota(jnp.int32, (SGU_CHUNK, SGU_WIDTH), 1) // SGU_GROUP_DIM
    chunks = []
    for c in range(tm // SGU_CHUNK):
        vc = vln[c * SGU_CHUNK:(c + 1) * SGU_CHUNK]
        acc = jnp.zeros((SGU_CHUNK, SGU_WIDTH), F32)
        for g in range(SGU_GROUPS):
            acc = jnp.where(lane_grp == g, _dot(w_tril[g], vc), acc)
        chunks.append(acc + bs_ref[...])
    sgu = (gu * jnp.concatenate(chunks, axis=0)).astype(BF16)
    y_sgu = _dot(sgu, wbs_ref[...])
    ms_out[0] = (_sigmoid(proj(_SEG_GSGU)) * y_sgu).astype(BF16)
    gm_out[0] = _sigmoid(proj(_SEG_GMLA)).astype(BF16)


def _pre(x, mod, positions, g1, w_in_p, qg, wqa_t, wqb_t, kvg, wk, wv_t, vones, fcol,
         vng, vnb, w_s, bs_full, wbs):
    batch, seq, _ = x.shape
    tm = TOKEN_TILE
    assert tm == ATTN_TILE and seq % tm == 0
    nt = seq // tm
    hw = MLA_HEADS * HEAD_PAD

    def const(shape):
        return pl.BlockSpec(shape, lambda b, i: (0,) * len(shape))

    return pl.pallas_call(
        _pre_kernel,
        grid=(batch, nt),
        in_specs=[
            pl.BlockSpec((1, tm, D_MODEL), lambda b, i: (b, i, 0)),
            pl.BlockSpec((1, N_MOD, D_MODEL), lambda b, i: (b, 0, 0)),
            pl.BlockSpec((1, 1, tm), lambda b, i: (b, 0, i)),
            const((1, D_MODEL)),
            const((D_MODEL, PACKED_IN)),
            const((1, Q_LORA_RANK)),
            const((hw, Q_LORA_RANK)),
            const((hw, Q_LORA_RANK)),
            const((1, KV_LORA_RANK)),
            const((KV_LORA_RANK, hw)),
            const((MLA_HEADS * V_ROWS, KV_LORA_RANK)),
            const((MLA_HEADS * V_ROWS, 1)),
            const((ROPE_HALF, 1)),
            const((1, SGU_WIDTH)),
            const((1, SGU_WIDTH)),
            const((SGU_GROUPS, SGU_CHUNK, SGU_CHUNK)),
            const((SGU_CHUNK, SGU_WIDTH)),
            const((SGU_WIDTH, D_MODEL)),
        ],
        out_specs=[
            pl.BlockSpec((1, MLA_HEADS, 1, HEAD_PAD, tm), lambda b, i: (b, 0, i, 0, 0)),
            pl.BlockSpec((1, MLA_HEADS, tm, HEAD_PAD), lambda b, i: (b, 0, i, 0)),
            pl.BlockSpec((1, MLA_HEADS, 1, V_ROWS, tm), lambda b, i: (b, 0, i, 0, 0)),
            pl.BlockSpec((1, tm, D_MODEL), lambda b, i: (b, i, 0)),
            pl.BlockSpec((1, tm, D_MODEL), lambda b, i: (b, i, 0)),
        ],
        out_shape=[
            jax.ShapeDtypeStruct((batch, MLA_HEADS, nt, HEAD_PAD, tm), BF16),
            jax.ShapeDtypeStruct((batch, MLA_HEADS, seq, HEAD_PAD), BF16),
            jax.ShapeDtypeStruct((batch, MLA_HEADS, nt, V_ROWS, tm), BF16),
            jax.ShapeDtypeStruct((batch, seq, D_MODEL), BF16),
            jax.ShapeDtypeStruct((batch, seq, D_MODEL), BF16),
        ],
        compiler_params=pltpu.CompilerParams(
            dimension_semantics=("arbitrary", "arbitrary"), vmem_limit_bytes=VMEM_LIMIT),
        name="pre",
    )(x, mod, positions.reshape(batch, 1, seq), g1, w_in_p, qg, wqa_t, wqb_t, kvg, wk, wv_t, vones,
      fcol, vng, vnb, w_s, bs_full, wbs)


def _flash_kernel(qt_ref, k_ref, vt_ref, o_ref, m_ref, acc_ref, s0_ref, s1_ref):
    t = qt_ref.shape[4]
    i = pl.program_id(2)
    m_ref[...] = jnp.full(m_ref.shape, -1e30, F32)
    acc_ref[...] = jnp.zeros(acc_ref.shape, F32)

    def logits(c, j):
        start = pl.multiple_of(j * t, t)
        return _dot(k_ref[0, 0, pl.ds(start, t), :], qt_ref[0, 0, c])

    def update(c, j, s_ref, masked):
        def read():
            s = s_ref[c]
            if masked:
                key = lax.broadcasted_iota(jnp.int32, (t, t), 0)
                qry = lax.broadcasted_iota(jnp.int32, (t, t), 1)
                s = jnp.where(key <= qry, s, -jnp.inf)
            return s
        m_old = m_ref[c]
        m_new = jnp.maximum(m_old, jnp.max(read(), axis=0, keepdims=True))
        p = jnp.exp2(read() - m_new).astype(BF16)
        pv = _dot(vt_ref[0, 0, j], p)
        acc_ref[c] = jnp.exp2(m_old - m_new) * acc_ref[c] + pv
        m_ref[c] = m_new

    bufs = (s0_ref, s1_ref)
    chains = list(range(ATTN_CHAINS))
    last = ATTN_CHAINS - 1
    first = ATTN_CHAINS * i

    for c in chains:
        s0_ref[c] = logits(c, 0)
    for c in chains[:last]:
        s1_ref[c] = logits(c, 1)

    def body(u, carry):
        base = ATTN_CHAINS * u
        s1_ref[last] = logits(last, base + 1)
        for kk in range(ATTN_CHAINS):
            for c in chains:
                update(c, base + kk, bufs[kk % 2], False)
                if not (kk == ATTN_CHAINS - 1 and c == last):
                    bufs[kk % 2][c] = logits(c, base + kk + 2)
        return carry

    lax.fori_loop(0, i, body, 0)
    s1_ref[last] = logits(last, first + 1)
    for d in range(ATTN_CHAINS):
        for c in chains[d:]:
            update(c, first + d, bufs[d % 2], c == d)
            if c >= d + 2:
                bufs[d % 2][c] = logits(c, first + d + 2)
    for c in range(ATTN_CHAINS):
        acc = acc_ref[c]
        o_t = jnp.concatenate([acc / acc[V_HEAD_DIM:V_HEAD_DIM + 1, :],
                               jnp.zeros((HEAD_PAD - V_ROWS, t), F32)], axis=0)
        o_ref[0, 0, c * t:(c + 1) * t, :] = o_t.T.astype(BF16)


def _flash(qt, k, vt):
    batch, heads, nt, _, t = qt.shape
    seq = nt * t
    assert nt % ATTN_CHAINS == 0 and ATTN_CHAINS % 2 == 0
    return pl.pallas_call(
        _flash_kernel,
        grid=(batch, heads, nt // ATTN_CHAINS),
        in_specs=[
            pl.BlockSpec((1, 1, ATTN_CHAINS, HEAD_PAD, t), lambda b, h, i: (b, h, i, 0, 0)),
            pl.BlockSpec((1, 1, seq, HEAD_PAD), lambda b, h, i: (b, h, 0, 0)),
            pl.BlockSpec((1, 1, nt, V_ROWS, t), lambda b, h, i: (b, h, 0, 0, 0)),
        ],
        out_specs=pl.BlockSpec((1, 1, ATTN_CHAINS * t, HEAD_PAD), lambda b, h, i: (b, h, i, 0)),
        out_shape=jax.ShapeDtypeStruct((batch, heads, seq, HEAD_PAD), BF16),
        scratch_shapes=[pltpu.VMEM((ATTN_CHAINS, 1, t), F32), pltpu.VMEM((ATTN_CHAINS, V_ROWS, t), F32),
                        pltpu.VMEM((ATTN_CHAINS, t, t), F32), pltpu.VMEM((ATTN_CHAINS, t, t), F32)],
        compiler_params=pltpu.CompilerParams(
            dimension_semantics=("arbitrary", "arbitrary", "arbitrary"),
            vmem_limit_bytes=VMEM_LIMIT),
        name="flash",
    )(qt, k, vt)


def _post_kernel(x_ref, o_ref, gm_ref, ms_ref, mod_ref, wbm_ref, wout_ref, g2_ref, rw_ref, rb_ref,
                 x2_out, h2_out, rt_out, rwt_out, cnt_out, carry_ref):
    tm = x_ref.shape[1]

    @pl.when((pl.program_id(0) == 0) & (pl.program_id(1) == 0))
    def _():
        carry_ref[...] = jnp.zeros(carry_ref.shape, F32)

    gate1 = mod_ref[0, 2:3, :]
    shift2 = mod_ref[0, 3:4, :]
    scale2 = mod_ref[0, 4:5, :]
    y_mla = _dot(o_ref[0, 0], wbm_ref[0])
    for h in range(1, MLA_HEADS):
        y_mla = y_mla + _dot(o_ref[0, h], wbm_ref[h])
    merged = gm_ref[0].astype(F32) * y_mla + ms_ref[0].astype(F32)
    x2 = x_ref[0] + gate1 * _dot(merged.astype(BF16), wout_ref[...])
    x2_out[0] = x2
    h2 = _rms(x2, g2_ref[...]) * (1.0 + scale2) + shift2
    h2_out[0] = h2.astype(BF16)

    hi = h2.astype(BF16)
    lo = (h2 - hi.astype(F32)).astype(BF16)
    a = _dot(hi, rw_ref[...])
    b = _dot(lo, rw_ref[:, 0:LANES])
    logits = a[:, 0:LANES] + a[:, LANES:2 * LANES] + b + rb_ref[...]

    lane = lax.broadcasted_iota(jnp.int32, (tm, LANES), 1).astype(F32)
    big = float(LANES)
    ninf = -jnp.inf
    gl = jnp.where(lane < N_GROUPS, logits, ninf)
    gmax = jnp.max(gl, axis=-1, keepdims=True)
    grp = jnp.min(jnp.where(gl == gmax, lane, big), axis=-1, keepdims=True)
    p_grp = 1.0 / jnp.sum(jnp.exp(gl - gmax), axis=-1, keepdims=True)
    first = N_GROUPS + EXPERTS_PER_GROUP * grp
    el = jnp.where((lane >= first) & (lane < first + EXPERTS_PER_GROUP), logits, ninf)
    m1 = jnp.max(el, axis=-1, keepdims=True)
    i1 = jnp.min(jnp.where(el == m1, lane, big), axis=-1, keepdims=True)
    el2 = jnp.where(lane == i1, ninf, el)
    m2 = jnp.max(el2, axis=-1, keepdims=True)
    i2 = jnp.min(jnp.where(el2 == m2, lane, big), axis=-1, keepdims=True)
    tt = jnp.exp(m2 - m1)
    w0 = p_grp / (1.0 + tt)
    w1 = p_grp * tt / (1.0 + tt)
    e0 = i1 - N_GROUPS
    e1 = i2 - N_GROUPS

    onehot = jnp.where((lane == e0) | (lane == e1), 1.0, 0.0)
    row = lax.broadcasted_iota(jnp.int32, (tm, tm), 0)
    col = lax.broadcasted_iota(jnp.int32, (tm, tm), 1)
    lower = jnp.where(col < row, 1.0, 0.0).astype(BF16)
    before = _dot(lower, onehot.astype(BF16)) + carry_ref[...]
    r0 = jnp.sum(jnp.where(lane == e0, before, 0.0), axis=-1, keepdims=True)
    r1 = jnp.sum(jnp.where(lane == e1, before, 0.0), axis=-1, keepdims=True)
    carry_ref[...] = carry_ref[...] + jnp.sum(onehot, axis=0, keepdims=True)
    cnt_out[...] = carry_ref[...]

    ri = jnp.where(lane == 0, e0, jnp.where(lane == 1, e1, jnp.where(lane == 2, r0, jnp.where(lane == 3, r1, 0.0))))
    rt_out[0, 0] = ri.T[0:ROUTE_ROWS]
    rwt_out[0] = jnp.where(lane == 0, w0, jnp.where(lane == 1, w1, 0.0))


def _post(x, o, gm, ms, mod, wbm, wout, g2, rw, rb):
    batch, seq, _ = x.shape
    tm = TOKEN_TILE

    def const(shape):
        return pl.BlockSpec(shape, lambda b, i: (0,) * len(shape))

    tok = lambda w: pl.BlockSpec((1, tm, w), lambda b, i: (b, i, 0))
    return pl.pallas_call(
        _post_kernel,
        grid=(batch, seq // tm),
        in_specs=[
            tok(D_MODEL),
            pl.BlockSpec((1, MLA_HEADS, tm, HEAD_PAD), lambda b, i: (b, 0, i, 0)),
            tok(D_MODEL),
            tok(D_MODEL),
            pl.BlockSpec((1, N_MOD, D_MODEL), lambda b, i: (b, 0, 0)),
            const((MLA_HEADS, HEAD_PAD, D_MODEL)),
            const((D_MODEL, D_MODEL)),
            const((1, D_MODEL)),
            const((D_MODEL, 2 * LANES)),
            const((1, LANES)),
        ],
        out_specs=[tok(D_MODEL), tok(D_MODEL),
                   pl.BlockSpec((1, 1, ROUTE_ROWS, tm), lambda b, i: (b, i, 0, 0)),
                   tok(LANES), const((1, LANES))],
        out_shape=[
            jax.ShapeDtypeStruct((batch, seq, D_MODEL), F32),
            jax.ShapeDtypeStruct((batch, seq, D_MODEL), BF16),
            jax.ShapeDtypeStruct((batch, seq // tm, ROUTE_ROWS, tm), F32),
            jax.ShapeDtypeStruct((batch, seq, LANES), F32),
            jax.ShapeDtypeStruct((1, LANES), F32),
        ],
        scratch_shapes=[pltpu.VMEM((1, LANES), F32)],
        compiler_params=pltpu.CompilerParams(
            dimension_semantics=("arbitrary", "arbitrary"), vmem_limit_bytes=VMEM_LIMIT),
        name="post",
    )(x, o, gm, ms, mod, wbm, wout, g2, rw, rb)


def _slot_map_kernel(dest_ref, default_ref, out_ref):
    del default_ref
    n_tok = dest_ref.shape[0] // TOP_K
    for kk in range(TOP_K):
        def put(tok, carry, kk=kk):
            out_ref[dest_ref[kk * n_tok + tok]] = tok
            return carry
        lax.fori_loop(0, n_tok, put, 0, unroll=SLOT_MAP_UNROLL)


def _slot_map(dest, n_slots):
    default = jnp.arange(n_slots, dtype=jnp.int32) % (dest.shape[0] // TOP_K)
    return pl.pallas_call(
        _slot_map_kernel,
        in_specs=[pl.BlockSpec(memory_space=pltpu.SMEM), pl.BlockSpec(memory_space=pltpu.SMEM)],
        out_specs=pl.BlockSpec(memory_space=pltpu.SMEM),
        out_shape=jax.ShapeDtypeStruct((n_slots,), jnp.int32),
        input_output_aliases={1: 0},
        name="slot_map",
    )(dest, default)


def _expert_kernel(be_ref, nr_ref, x_ref, w1_ref, w3_ref, w2_ref, y_ref):
    j = pl.program_id(0)

    @pl.when(j < nr_ref[0])
    def _():
        x = x_ref[...]
        a = _dot(x, w1_ref[0, 0].astype(BF16))
        b = _dot(x, w3_ref[0, 0].astype(BF16))
        act = (a * _sigmoid(a) * b).astype(BF16)
        y_ref[...] = _dot(act, w2_ref[0, 0].astype(BF16)).astype(BF16)

    @pl.when(j >= nr_ref[0])
    def _():
        y_ref[...] = jnp.zeros(y_ref.shape, BF16)


def _experts(blk_e, n_real, x_sorted, w1, w3, w2):
    rows = x_sorted.shape[0]
    nblk = rows // EXPERT_BLOCK
    last = lambda j, be, nr: jnp.minimum(j, nr[0] - 1)
    w_map = lambda j, be, nr: (0, be[last(j, be, nr)], 0, 0)
    return pl.pallas_call(
        _expert_kernel,
        grid_spec=pltpu.PrefetchScalarGridSpec(
            num_scalar_prefetch=2,
            grid=(nblk,),
            in_specs=[
                pl.BlockSpec((EXPERT_BLOCK, D_MODEL), lambda j, be, nr: (last(j, be, nr), 0)),
                pl.BlockSpec((1, 1, D_MODEL, EXPERT_FF), w_map),
                pl.BlockSpec((1, 1, D_MODEL, EXPERT_FF), w_map),
                pl.BlockSpec((1, 1, EXPERT_FF, D_MODEL), w_map),
            ],
            out_specs=pl.BlockSpec((EXPERT_BLOCK, D_MODEL), lambda j, be, nr: (j, 0)),
        ),
        out_shape=jax.ShapeDtypeStruct((rows, D_MODEL), BF16),
        compiler_params=pltpu.CompilerParams(
            dimension_semantics=("arbitrary",), vmem_limit_bytes=VMEM_LIMIT),
        name="experts",
    )(blk_e, n_real, x_sorted, w1, w3, w2)


def _final_kernel(x2_ref, y0_ref, y1_ref, rwt_ref, mod_ref, fg_ref, out_ref):
    gate2 = mod_ref[0, 5:6, :]
    rwt = rwt_ref[0]
    moe = rwt[:, 0:1] * y0_ref[0].astype(F32) + rwt[:, 1:2] * y1_ref[0].astype(F32)
    x3 = x2_ref[0] + gate2 * moe
    out_ref[0] = _rms(x3, fg_ref[...])


def _final(x2, y0, y1, rwt, mod, fg):
    batch, seq, _ = x2.shape
    tm = TOKEN_TILE
    tok = lambda w: pl.BlockSpec((1, tm, w), lambda b, i: (b, i, 0))
    return pl.pallas_call(
        _final_kernel,
        grid=(batch, seq // tm),
        in_specs=[
            tok(D_MODEL), tok(D_MODEL), tok(D_MODEL), tok(LANES),
            pl.BlockSpec((1, N_MOD, D_MODEL), lambda b, i: (b, 0, 0)),
            pl.BlockSpec((1, D_MODEL), lambda b, i: (0, 0)),
        ],
        out_specs=tok(D_MODEL),
        out_shape=jax.ShapeDtypeStruct((batch, seq, D_MODEL), F32),
        compiler_params=pltpu.CompilerParams(
            dimension_semantics=("arbitrary", "arbitrary"), vmem_limit_bytes=VMEM_LIMIT),
        name="final",
    )(x2, y0, y1, rwt, mod, fg)


def _pack_weights(w_in, w_uq, w_ukv, w_br_mla, w_rg, b_rg, w_re, b_re, b_s):
    kr_end = Q_LORA_RANK + KV_LORA_RANK + QK_ROPE_DIM
    w_in_p = jnp.concatenate(
        [w_in[:, :kr_end], jnp.zeros((D_MODEL, _SEG_KR[1] - kr_end), F32), w_in[:, kr_end:]],
        axis=1).astype(BF16)
    pad = HEAD_PAD - QK_DIM
    hw = MLA_HEADS * HEAD_PAD
    wq = w_uq.reshape(Q_LORA_RANK, MLA_HEADS, QK_DIM)
    nope, r1, r2 = wq[..., :QK_NOPE_DIM], wq[..., QK_NOPE_DIM:QK_NOPE_DIM + ROPE_HALF], wq[..., QK_NOPE_DIM + ROPE_HALF:]
    zpad = jnp.zeros((Q_LORA_RANK, MLA_HEADS, pad), F32)
    wqa_t = jnp.concatenate([nope, r1, r2, zpad], axis=-1).reshape(Q_LORA_RANK, hw).T.astype(BF16)
    wqb_t = jnp.concatenate([jnp.zeros_like(nope), r2, r1, zpad], axis=-1).reshape(Q_LORA_RANK, hw).T.astype(BF16)
    wkv = w_ukv.reshape(KV_LORA_RANK, MLA_HEADS, QK_NOPE_DIM + V_HEAD_DIM)
    wk = jnp.concatenate([wkv[..., :QK_NOPE_DIM], jnp.zeros((KV_LORA_RANK, MLA_HEADS, HEAD_PAD - QK_NOPE_DIM), F32)],
                         axis=-1).reshape(KV_LORA_RANK, hw).astype(BF16)
    wv_t = jnp.concatenate([wkv[..., QK_NOPE_DIM:], jnp.zeros((KV_LORA_RANK, MLA_HEADS, V_ROWS - V_HEAD_DIM), F32)],
                           axis=-1).reshape(KV_LORA_RANK, MLA_HEADS * V_ROWS).T.astype(BF16)
    vones = jnp.zeros((MLA_HEADS, V_ROWS), F32).at[:, V_HEAD_DIM].set(1.0).reshape(MLA_HEADS * V_ROWS, 1)
    wbm = jnp.concatenate(
        [w_br_mla.reshape(MLA_HEADS, V_HEAD_DIM, D_MODEL),
         jnp.zeros((MLA_HEADS, HEAD_PAD - V_HEAD_DIM, D_MODEL), F32)], axis=1).astype(BF16)
    n_r = N_GROUPS + N_EXPERTS
    wr = jnp.concatenate([w_rg, w_re, jnp.zeros((D_MODEL, LANES - n_r), F32)], axis=1)
    wr_hi = wr.astype(BF16)
    wr_lo = (wr - wr_hi.astype(F32)).astype(BF16)
    rw = jnp.concatenate([wr_hi, wr_lo], axis=1)
    rb = jnp.concatenate([b_rg, b_re, jnp.zeros((LANES - n_r,), F32)]).reshape(1, LANES)
    bs_full = jnp.repeat(b_s.T, SGU_GROUP_DIM, axis=1)
    fcol = (ROPE_THETA ** (-jnp.arange(ROPE_HALF, dtype=F32) / ROPE_HALF)).reshape(ROPE_HALF, 1)
    return w_in_p, wqa_t, wqb_t, wk, wv_t, vones, wbm, rw, rb, bs_full, fcol


def kernel(x, c, positions, w_ada, b_ada, norm1_g, w_in, q_norm_g, w_uq, kv_norm_g, w_ukv, v_norm_g, v_norm_b, w_s, b_s, w_br_mla, w_br_sgu, w_out, norm2_g, w_rg, b_rg, w_re, b_re, w1, w3, w2, final_g):
    batch, seq, _ = x.shape
    n_tok = batch * seq
    assert w_ada.shape[0] == 1, "single-layer block"
    l = 0
    row = lambda v: v.reshape(1, -1)
    (w_in_p, wqa_t, wqb_t, wk, wv_t, vones, wbm, rw, rb, bs_full, fcol) = _pack_weights(
        w_in[l], w_uq[l], w_ukv[l], w_br_mla[l], w_rg[l], b_rg[l], w_re[l], b_re[l], b_s[l])
    mod = _adaln(c, w_ada[l], b_ada[l]).reshape(batch, N_MOD, D_MODEL)
    qt, k, vt, gm, ms = _pre(
        x, mod, positions, row(norm1_g[l]), w_in_p, row(q_norm_g[l]), wqa_t, wqb_t, row(kv_norm_g[l]),
        wk, wv_t, vones, fcol, row(v_norm_g[l]), row(v_norm_b[l]), w_s[l], bs_full,
        w_br_sgu[l].astype(BF16))
    o = _flash(qt, k, vt)
    x2, h2, rt, rwt, cnt = _post(x, o, gm, ms, mod, wbm, w_out[l].astype(BF16), row(norm2_g[l]), rw, rb)

    counts = cnt[0, :N_EXPERTS].astype(jnp.int32)
    padded = ((counts + EXPERT_BLOCK - 1) // EXPERT_BLOCK) * EXPERT_BLOCK
    pad_ends = jnp.cumsum(padded)
    pad_starts = pad_ends - padded
    n_assign = n_tok * TOP_K
    rows_total = ((n_assign + EXPERT_BLOCK - 1) // EXPERT_BLOCK + N_EXPERTS) * EXPERT_BLOCK
    nblk = rows_total // EXPERT_BLOCK
    rti = rt.astype(jnp.int32)
    expert_ids = jnp.arange(N_EXPERTS, dtype=jnp.int32)
    dests = []
    for kk in range(TOP_K):
        e_k = rti[:, :, kk, :].reshape(n_tok)
        r_k = rti[:, :, TOP_K + kk, :].reshape(n_tok)
        start_k = jnp.sum(jnp.where(e_k[:, None] == expert_ids[None, :], pad_starts[None, :], 0), axis=1)
        dests.append(start_k + r_k)
    slot_tok = _slot_map(jnp.concatenate(dests), rows_total)
    blk_start = jnp.arange(nblk, dtype=jnp.int32) * EXPERT_BLOCK
    blk_e = jnp.minimum(jnp.sum((pad_ends[None, :] <= blk_start[:, None]).astype(jnp.int32), axis=1),
                        N_EXPERTS - 1)
    n_real = pad_ends[-1:] // EXPERT_BLOCK
    x_sorted = h2.reshape(n_tok, D_MODEL)[slot_tok]
    y = _experts(blk_e, n_real, x_sorted, w1, w3, w2)
    y0, y1 = [y[d].reshape(batch, seq, D_MODEL) for d in dests]
    return _final(x2, y0, y1, rwt, mod, row(final_g))
```

```python
import jax
import jax.numpy as jnp
from jax import lax
from jax.experimental import pallas as pl
from jax.experimental.pallas import tpu as pltpu

F32 = jnp.float32
BF16 = jnp.bfloat16

D_MODEL = 1024
MLA_HEADS = 8
QK_NOPE_DIM = 64
QK_ROPE_DIM = 32
ROPE_HALF = QK_ROPE_DIM // 2
QK_DIM = QK_NOPE_DIM + QK_ROPE_DIM
V_HEAD_DIM = 64
Q_LORA_RANK = 256
KV_LORA_RANK = 128
ROPE_THETA = 10000.0
SGU_GROUPS = 8
SGU_GROUP_DIM = 64
SGU_WIDTH = SGU_GROUPS * SGU_GROUP_DIM
SGU_CHUNK = 128
N_GROUPS = 4
EXPERTS_PER_GROUP = 8
N_EXPERTS = N_GROUPS * EXPERTS_PER_GROUP
TOP_K = 2
EXPERT_FF = 512
N_MOD = 6
EPS = 1e-6

LANES = 128
HEAD_PAD = LANES
BF16_SUBLANES = 16
V_ROWS = -(-(V_HEAD_DIM + 1) // BF16_SUBLANES) * BF16_SUBLANES
VMEM_LIMIT = 48 * 1024 * 1024

TOKEN_TILE = 512
ATTN_TILE = 512
ATTN_CHAINS = 4
EXPERT_BLOCK = 512
ADALN_TILE = 512
ROUTE_ROWS = 8

_SEG_CQ = (0, 256)
_SEG_CKV = (256, 384)
_SEG_KR = (384, 512)
_SEG_U = (512, 1024)
_SEG_V = (1024, 1536)
_SEG_GMLA = (1536, 2560)
_SEG_GSGU = (2560, 3584)
PACKED_IN = 3584

Q_SCALE = (QK_DIM ** -0.5) * 1.4426950408889634

_NT_DIMS = (((1,), (1,)), ((), ()))


def _sigmoid(x):
    return 1.0 / (1.0 + jnp.exp(-x))


def _gelu_tanh(x):
    c = 0.7978845608028654
    return 0.5 * x * (1.0 + jnp.tanh(c * (x + 0.044715 * (x * x * x))))


def _rms(x, g):
    return x * lax.rsqrt(jnp.mean(x * x, axis=-1, keepdims=True) + EPS) * g


def _dot(a, b):
    return jnp.dot(a, b, preferred_element_type=F32)


def _dot_nt(a, b):
    return lax.dot_general(a, b, _NT_DIMS, preferred_element_type=F32)


def _adaln_kernel(ct_ref, w_ref, b_ref, o_ref):
    ct = ct_ref[...]
    s = ct * _sigmoid(ct)
    w = w_ref[...]
    rows = [jnp.sum(s[:, b:b + 1] * w, axis=0, keepdims=True) for b in range(ct.shape[1])]
    o_ref[...] = jnp.concatenate(rows, axis=0) + b_ref[...]


def _adaln(c, w_ada, b_ada):
    batch = c.shape[0]
    width = w_ada.shape[1]
    return pl.pallas_call(
        _adaln_kernel,
        grid=(width // ADALN_TILE,),
        in_specs=[
            pl.BlockSpec((D_MODEL, batch), lambda j: (0, 0)),
            pl.BlockSpec((D_MODEL, ADALN_TILE), lambda j: (0, j)),
            pl.BlockSpec((1, ADALN_TILE), lambda j: (0, j)),
        ],
        out_specs=pl.BlockSpec((batch, ADALN_TILE), lambda j: (0, j)),
        out_shape=jax.ShapeDtypeStruct((batch, width), F32),
        compiler_params=pltpu.CompilerParams(dimension_semantics=("arbitrary",)),
        name="adaln",
    )(c.T, w_ada, b_ada.reshape(1, width))


def _pre_kernel(x_ref, mod_ref, pos_ref, g1_ref, win_ref, qg_ref, wqa_ref, wqb_ref,
                kvg_ref, wk_ref, wv_ref, vones_ref, fcol_ref,
                vng_ref, vnb_ref, ws_ref, bs_ref, wbs_ref,
                qt_out, k_out, vt_out, gm_out, ms_out):
    tm = x_ref.shape[1]
    x = x_ref[0]
    shift1 = mod_ref[0, 0:1, :]
    scale1 = mod_ref[0, 1:2, :]
    h1 = _rms(x, g1_ref[...]) * (1.0 + scale1) + shift1
    hb = h1.astype(BF16)

    def proj(seg):
        return _dot(hb, win_ref[:, seg[0]:seg[1]])

    pos = pos_ref[0].astype(F32)
    ang = fcol_ref[...] * pos
    cos_t = jnp.cos(ang)
    sin_t = jnp.sin(ang)
    pad_rows = HEAD_PAD - QK_DIM

    cqn = _rms(proj(_SEG_CQ), qg_ref[...]).astype(BF16)
    qa_t = _dot_nt(wqa_ref[...], cqn)
    qb_t = _dot_nt(wqb_ref[...], cqn)
    c_tab = jnp.concatenate([jnp.ones((QK_NOPE_DIM, tm), F32), cos_t, cos_t,
                             jnp.ones((pad_rows, tm), F32)], axis=0) * Q_SCALE
    s_tab = jnp.concatenate([jnp.zeros((QK_NOPE_DIM, tm), F32), -sin_t, sin_t,
                             jnp.zeros((pad_rows, tm), F32)], axis=0) * Q_SCALE
    for h in range(MLA_HEADS):
        sl = slice(h * HEAD_PAD, (h + 1) * HEAD_PAD)
        qt_out[0, h, 0] = (qa_t[sl] * c_tab + qb_t[sl] * s_tab).astype(BF16)

    ckvn = _rms(proj(_SEG_CKV), kvg_ref[...]).astype(BF16)
    kr_t = proj(_SEG_KR).T
    x1 = kr_t[0:ROPE_HALF]
    x2 = kr_t[ROPE_HALF:QK_ROPE_DIM]
    k_pe = jnp.concatenate([jnp.zeros((QK_NOPE_DIM, tm), F32), x1 * cos_t - x2 * sin_t,
                            x1 * sin_t + x2 * cos_t, jnp.zeros((pad_rows, tm), F32)], axis=0).T
    k_full = _dot(ckvn, wk_ref[...])
    v_t = _dot_nt(wv_ref[...], ckvn) + vones_ref[...]
    for h in range(MLA_HEADS):
        sl = slice(h * HEAD_PAD, (h + 1) * HEAD_PAD)
        k_out[0, h] = (k_full[:, sl] + k_pe).astype(BF16)
        vt_out[0, h, 0] = v_t[h * V_ROWS:(h + 1) * V_ROWS].astype(BF16)

    gu = _gelu_tanh(proj(_SEG_U))
    gv = _gelu_tanh(proj(_SEG_V))
    mu = jnp.mean(gv, axis=-1, keepdims=True)
    dv = gv - mu
    var = jnp.mean(dv * dv, axis=-1, keepdims=True)
    vln = (dv * lax.rsqrt(var + EPS) * vng_ref[...] + vnb_ref[...]).astype(BF16)
    row = lax.broadcasted_iota(jnp.int32, (SGU_CHUNK, SGU_CHUNK), 0)
    col = lax.broadcasted_iota(jnp.int32, (SGU_CHUNK, SGU_CHUNK), 1)
    causal = col <= row
    w_tril = [jnp.where(causal, ws_ref[g], 0.0).astype(BF16) for g in range(SGU_GROUPS)]
    lane_grp = lax.broadcasted_iota(jnp.int32, (SGU_CHUNK, SGU_WIDTH), 1) // SGU_GROUP_DIM
    chunks = []
    for c in range(tm // SGU_CHUNK):
        vc = vln[c * SGU_CHUNK:(c + 1) * SGU_CHUNK]
        acc = jnp.zeros((SGU_CHUNK, SGU_WIDTH), F32)
        for g in range(SGU_GROUPS):
            acc = jnp.where(lane_grp == g, _dot(w_tril[g], vc), acc)
        chunks.append(acc + bs_ref[...])
    sgu = (gu * jnp.concatenate(chunks, axis=0)).astype(BF16)
    y_sgu = _dot(sgu, wbs_ref[...])
    ms_out[0] = (_sigmoid(proj(_SEG_GSGU)) * y_sgu).astype(BF16)
    gm_out[0] = _sigmoid(proj(_SEG_GMLA)).astype(BF16)


def _pre(x, mod, positions, g1, w_in_p, qg, wqa_t, wqb_t, kvg, wk, wv_t, vones, fcol,
         vng, vnb, w_s, bs_full, wbs):
    batch, seq, _ = x.shape
    tm = TOKEN_TILE
    assert tm == ATTN_TILE and seq % tm == 0
    nt = seq // tm
    hw = MLA_HEADS * HEAD_PAD

    def const(shape):
        return pl.BlockSpec(shape, lambda b, i: (0,) * len(shape))

    return pl.pallas_call(
        _pre_kernel,
        grid=(batch, nt),
        in_specs=[
            pl.BlockSpec((1, tm, D_MODEL), lambda b, i: (b, i, 0)),
            pl.BlockSpec((1, N_MOD, D_MODEL), lambda b, i: (b, 0, 0)),
            pl.BlockSpec((1, 1, tm), lambda b, i: (b, 0, i)),
            const((1, D_MODEL)),
            const((D_MODEL, PACKED_IN)),
            const((1, Q_LORA_RANK)),
            const((hw, Q_LORA_RANK)),
            const((hw, Q_LORA_RANK)),
            const((1, KV_LORA_RANK)),
            const((KV_LORA_RANK, hw)),
            const((MLA_HEADS * V_ROWS, KV_LORA_RANK)),
            const((MLA_HEADS * V_ROWS, 1)),
            const((ROPE_HALF, 1)),
            const((1, SGU_WIDTH)),
            const((1, SGU_WIDTH)),
            const((SGU_GROUPS, SGU_CHUNK, SGU_CHUNK)),
            const((SGU_CHUNK, SGU_WIDTH)),
            const((SGU_WIDTH, D_MODEL)),
        ],
        out_specs=[
            pl.BlockSpec((1, MLA_HEADS, 1, HEAD_PAD, tm), lambda b, i: (b, 0, i, 0, 0)),
            pl.BlockSpec((1, MLA_HEADS, tm, HEAD_PAD), lambda b, i: (b, 0, i, 0)),
            pl.BlockSpec((1, MLA_HEADS, 1, V_ROWS, tm), lambda b, i: (b, 0, i, 0, 0)),
            pl.BlockSpec((1, tm, D_MODEL), lambda b, i: (b, i, 0)),
            pl.BlockSpec((1, tm, D_MODEL), lambda b, i: (b, i, 0)),
        ],
        out_shape=[
            jax.ShapeDtypeStruct((batch, MLA_HEADS, nt, HEAD_PAD, tm), BF16),
            jax.ShapeDtypeStruct((batch, MLA_HEADS, seq, HEAD_PAD), BF16),
            jax.ShapeDtypeStruct((batch, MLA_HEADS, nt, V_ROWS, tm), BF16),
            jax.ShapeDtypeStruct((batch, seq, D_MODEL), BF16),
            jax.ShapeDtypeStruct((batch, seq, D_MODEL), BF16),
        ],
        compiler_params=pltpu.CompilerParams(
            dimension_semantics=("arbitrary", "arbitrary"), vmem_limit_bytes=VMEM_LIMIT),
        name="pre",
    )(x, mod, positions.reshape(batch, 1, seq), g1, w_in_p, qg, wqa_t, wqb_t, kvg, wk, wv_t, vones,
      fcol, vng, vnb, w_s, bs_full, wbs)


def _flash_kernel(qt_ref, k_ref, vt_ref, o_ref, m_ref, acc_ref, s0_ref, s1_ref):
    t = qt_ref.shape[4]
    i = pl.program_id(2)
    m_ref[...] = jnp.full(m_ref.shape, -1e30, F32)
    acc_ref[...] = jnp.zeros(acc_ref.shape, F32)

    def logits(c, j):
        start = pl.multiple_of(j * t, t)
        return _dot(k_ref[0, 0, pl.ds(start, t), :], qt_ref[0, 0, c])

    def update(c, j, s_ref, masked):
        def read():
            s = s_ref[c]
            if masked:
                key = lax.broadcasted_iota(jnp.int32, (t, t), 0)
                qry = lax.broadcasted_iota(jnp.int32, (t, t), 1)
                s = jnp.where(key <= qry, s, -jnp.inf)
            return s
        m_old = m_ref[c]
        m_new = jnp.maximum(m_old, jnp.max(read(), axis=0, keepdims=True))
        p = jnp.exp2(read() - m_new).astype(BF16)
        pv = _dot(vt_ref[0, 0, j], p)
        acc_ref[c] = jnp.exp2(m_old - m_new) * acc_ref[c] + pv
        m_ref[c] = m_new

    bufs = (s0_ref, s1_ref)
    chains = list(range(ATTN_CHAINS))
    last = ATTN_CHAINS - 1
    first = ATTN_CHAINS * i

    for c in chains:
        s0_ref[c] = logits(c, 0)
    for c in chains[:last]:
        s1_ref[c] = logits(c, 1)

    def body(u, carry):
        base = ATTN_CHAINS * u
        s1_ref[last] = logits(last, base + 1)
        for kk in range(ATTN_CHAINS):
            for c in chains:
                update(c, base + kk, bufs[kk % 2], False)
                if not (kk == ATTN_CHAINS - 1 and c == last):
                    bufs[kk % 2][c] = logits(c, base + kk + 2)
        return carry

    lax.fori_loop(0, i, body, 0)
    s1_ref[last] = logits(last, first + 1)
    for d in range(ATTN_CHAINS):
        for c in chains[d:]:
            update(c, first + d, bufs[d % 2], c == d)
            if c >= d + 2:
                bufs[d % 2][c] = logits(c, first + d + 2)
    for c in range(ATTN_CHAINS):
        acc = acc_ref[c]
        o_t = jnp.concatenate([acc / acc[V_HEAD_DIM:V_HEAD_DIM + 1, :],
                               jnp.zeros((HEAD_PAD - V_ROWS, t), F32)], axis=0)
        o_ref[0, 0, c * t:(c + 1) * t, :] = o_t.T.astype(BF16)


def _flash(qt, k, vt):
    batch, heads, nt, _, t = qt.shape
    seq = nt * t
    assert nt % ATTN_CHAINS == 0 and ATTN_CHAINS % 2 == 0
    return pl.pallas_call(
        _flash_kernel,
        grid=(batch, heads, nt // ATTN_CHAINS),
        in_specs=[
            pl.BlockSpec((1, 1, ATTN_CHAINS, HEAD_PAD, t), lambda b, h, i: (b, h, i, 0, 0)),
            pl.BlockSpec((1, 1, seq, HEAD_PAD), lambda b, h, i: (b, h, 0, 0)),
            pl.BlockSpec((1, 1, nt, V_ROWS, t), lambda b, h, i: (b, h, 0, 0, 0)),
        ],
        out_specs=pl.BlockSpec((1, 1, ATTN_CHAINS * t, HEAD_PAD), lambda b, h, i: (b, h, i, 0)),
        out_shape=jax.ShapeDtypeStruct((batch, heads, seq, HEAD_PAD), BF16),
        scratch_shapes=[pltpu.VMEM((ATTN_CHAINS, 1, t), F32), pltpu.VMEM((ATTN_CHAINS, V_ROWS, t), F32),
                        pltpu.VMEM((ATTN_CHAINS, t, t), F32), pltpu.VMEM((ATTN_CHAINS, t, t), F32)],
        compiler_params=pltpu.CompilerParams(
            dimension_semantics=("arbitrary", "arbitrary", "arbitrary"),
            vmem_limit_bytes=VMEM_LIMIT),
        name="flash",
    )(qt, k, vt)


def _post_kernel(x_ref, o_ref, gm_ref, ms_ref, mod_ref, wbm_ref, wout_ref, g2_ref, rw_ref, rb_ref,
                 x2_out, h2_out, rt_out, rwt_out, cnt_out, carry_ref):
    tm = x_ref.shape[1]

    @pl.when((pl.program_id(0) == 0) & (pl.program_id(1) == 0))
    def _():
        carry_ref[...] = jnp.zeros(carry_ref.shape, F32)

    gate1 = mod_ref[0, 2:3, :]
    shift2 = mod_ref[0, 3:4, :]
    scale2 = mod_ref[0, 4:5, :]
    y_mla = _dot(o_ref[0, 0], wbm_ref[0])
    for h in range(1, MLA_HEADS):
        y_mla = y_mla + _dot(o_ref[0, h], wbm_ref[h])
    merged = gm_ref[0].astype(F32) * y_mla + ms_ref[0].astype(F32)
    x2 = x_ref[0] + gate1 * _dot(merged.astype(BF16), wout_ref[...])
    x2_out[0] = x2
    h2 = _rms(x2, g2_ref[...]) * (1.0 + scale2) + shift2
    h2_out[0] = h2.astype(BF16)

    hi = h2.astype(BF16)
    lo = (h2 - hi.astype(F32)).astype(BF16)
    a = _dot(hi, rw_ref[...])
    b = _dot(lo, rw_ref[:, 0:LANES])
    logits = a[:, 0:LANES] + a[:, LANES:2 * LANES] + b + rb_ref[...]

    lane = lax.broadcasted_iota(jnp.int32, (tm, LANES), 1).astype(F32)
    big = float(LANES)
    ninf = -jnp.inf
    gl = jnp.where(lane < N_GROUPS, logits, ninf)
    gmax = jnp.max(gl, axis=-1, keepdims=True)
    grp = jnp.min(jnp.where(gl == gmax, lane, big), axis=-1, keepdims=True)
    p_grp = 1.0 / jnp.sum(jnp.exp(gl - gmax), axis=-1, keepdims=True)
    first = N_GROUPS + EXPERTS_PER_GROUP * grp
    el = jnp.where((lane >= first) & (lane < first + EXPERTS_PER_GROUP), logits, ninf)
    m1 = jnp.max(el, axis=-1, keepdims=True)
    i1 = jnp.min(jnp.where(el == m1, lane, big), axis=-1, keepdims=True)
    el2 = jnp.where(lane == i1, ninf, el)
    m2 = jnp.max(el2, axis=-1, keepdims=True)
    i2 = jnp.min(jnp.where(el2 == m2, lane, big), axis=-1, keepdims=True)
    tt = jnp.exp(m2 - m1)
    w0 = p_grp / (1.0 + tt)
    w1 = p_grp * tt / (1.0 + tt)
    e0 = i1 - N_GROUPS
    e1 = i2 - N_GROUPS

    onehot = jnp.where((lane == e0) | (lane == e1), 1.0, 0.0)
    row = lax.broadcasted_iota(jnp.int32, (tm, tm), 0)
    col = lax.broadcasted_iota(jnp.int32, (tm, tm), 1)
    lower = jnp.where(col < row, 1.0, 0.0).astype(BF16)
    before = _dot(lower, onehot.astype(BF16)) + carry_ref[...]
    r0 = jnp.sum(jnp.where(lane == e0, before, 0.0), axis=-1, keepdims=True)
    r1 = jnp.sum(jnp.where(lane == e1, before, 0.0), axis=-1, keepdims=True)
    carry_ref[...] = carry_ref[...] + jnp.sum(onehot, axis=0, keepdims=True)
    cnt_out[...] = carry_ref[...]

    ri = jnp.where(lane == 0, e0, jnp.where(lane == 1, e1, jnp.where(lane == 2, r0, jnp.where(lane == 3, r1, 0.0))))
    rt_out[0, 0] = ri.T[0:ROUTE_ROWS]
    rwt_out[0] = jnp.where(lane == 0, w0, jnp.where(lane == 1, w1, 0.0))


def _post(x, o, gm, ms, mod, wbm, wout, g2, rw, rb):
    batch, seq, _ = x.shape
    tm = TOKEN_TILE

    def const(shape):
        return pl.BlockSpec(shape, lambda b, i: (0,) * len(shape))

    tok = lambda w: pl.BlockSpec((1, tm, w), lambda b, i: (b, i, 0))
    return pl.pallas_call(
        _post_kernel,
        grid=(batch, seq // tm),
        in_specs=[
            tok(D_MODEL),
            pl.BlockSpec((1, MLA_HEADS, tm, HEAD_PAD), lambda b, i: (b, 0, i, 0)),
            tok(D_MODEL),
            tok(D_MODEL),
            pl.BlockSpec((1, N_MOD, D_MODEL), lambda b, i: (b, 0, 0)),
            const((MLA_HEADS, HEAD_PAD, D_MODEL)),
            const((D_MODEL, D_MODEL)),
            const((1, D_MODEL)),
            const((D_MODEL, 2 * LANES)),
            const((1, LANES)),
        ],
        out_specs=[tok(D_MODEL), tok(D_MODEL),
                   pl.BlockSpec((1, 1, ROUTE_ROWS, tm), lambda b, i: (b, i, 0, 0)),
                   tok(LANES), const((1, LANES))],
        out_shape=[
            jax.ShapeDtypeStruct((batch, seq, D_MODEL), F32),
            jax.ShapeDtypeStruct((batch, seq, D_MODEL), BF16),
            jax.ShapeDtypeStruct((batch, seq // tm, ROUTE_ROWS, tm), F32),
            jax.ShapeDtypeStruct((batch, seq, LANES), F32),
            jax.ShapeDtypeStruct((1, LANES), F32),
        ],
        scratch_shapes=[pltpu.VMEM((1, LANES), F32)],
        compiler_params=pltpu.CompilerParams(
            dimension_semantics=("arbitrary", "arbitrary"), vmem_limit_bytes=VMEM_LIMIT),
        name="post",
    )(x, o, gm, ms, mod, wbm, wout, g2, rw, rb)


def _expert_kernel(be_ref, nr_ref, x_ref, w1_ref, w3_ref, w2_ref, y_ref):
    j = pl.program_id(0)

    @pl.when(j < nr_ref[0])
    def _():
        x = x_ref[...]
        a = _dot(x, w1_ref[0, 0].astype(BF16))
        b = _dot(x, w3_ref[0, 0].astype(BF16))
        act = (a * _sigmoid(a) * b).astype(BF16)
        y_ref[...] = _dot(act, w2_ref[0, 0].astype(BF16)).astype(BF16)

    @pl.when(j >= nr_ref[0])
    def _():
        y_ref[...] = jnp.zeros(y_ref.shape, BF16)


def _experts(blk_e, n_real, x_sorted, w1, w3, w2):
    rows = x_sorted.shape[0]
    nblk = rows // EXPERT_BLOCK
    last = lambda j, be, nr: jnp.minimum(j, nr[0] - 1)
    w_map = lambda j, be, nr: (0, be[last(j, be, nr)], 0, 0)
    return pl.pallas_call(
        _expert_kernel,
        grid_spec=pltpu.PrefetchScalarGridSpec(
            num_scalar_prefetch=2,
            grid=(nblk,),
            in_specs=[
                pl.BlockSpec((EXPERT_BLOCK, D_MODEL), lambda j, be, nr: (last(j, be, nr), 0)),
                pl.BlockSpec((1, 1, D_MODEL, EXPERT_FF), w_map),
                pl.BlockSpec((1, 1, D_MODEL, EXPERT_FF), w_map),
                pl.BlockSpec((1, 1, EXPERT_FF, D_MODEL), w_map),
            ],
            out_specs=pl.BlockSpec((EXPERT_BLOCK, D_MODEL), lambda j, be, nr: (j, 0)),
        ),
        out_shape=jax.ShapeDtypeStruct((rows, D_MODEL), BF16),
        compiler_params=pltpu.CompilerParams(
            dimension_semantics=("arbitrary",), vmem_limit_bytes=VMEM_LIMIT),
        name="experts",
    )(blk_e, n_real, x_sorted, w1, w3, w2)


def _final_kernel(x2_ref, y0_ref, y1_ref, rwt_ref, mod_ref, fg_ref, out_ref):
    gate2 = mod_ref[0, 5:6, :]
    rwt = rwt_ref[0]
    moe = rwt[:, 0:1] * y0_ref[0].astype(F32) + rwt[:, 1:2] * y1_ref[0].astype(F32)
    x3 = x2_ref[0] + gate2 * moe
    out_ref[0] = _rms(x3, fg_ref[...])


def _final(x2, y0, y1, rwt, mod, fg):
    batch, seq, _ = x2.shape
    tm = TOKEN_TILE
    tok = lambda w: pl.BlockSpec((1, tm, w), lambda b, i: (b, i, 0))
    return pl.pallas_call(
        _final_kernel,
        grid=(batch, seq // tm),
        in_specs=[
            tok(D_MODEL), tok(D_MODEL), tok(D_MODEL), tok(LANES),
            pl.BlockSpec((1, N_MOD, D_MODEL), lambda b, i: (b, 0, 0)),
            pl.BlockSpec((1, D_MODEL), lambda b, i: (0, 0)),
        ],
        out_specs=tok(D_MODEL),
        out_shape=jax.ShapeDtypeStruct((batch, seq, D_MODEL), F32),
        compiler_params=pltpu.CompilerParams(
            dimension_semantics=("arbitrary", "arbitrary"), vmem_limit_bytes=VMEM_LIMIT),
        name="final",
    )(x2, y0, y1, rwt, mod, fg)


def _pack_weights(w_in, w_uq, w_ukv, w_br_mla, w_rg, b_rg, w_re, b_re, b_s):
    kr_end = Q_LORA_RANK + KV_LORA_RANK + QK_ROPE_DIM
    w_in_p = jnp.concatenate(
        [w_in[:, :kr_end], jnp.zeros((D_MODEL, _SEG_KR[1] - kr_end), F32), w_in[:, kr_end:]],
        axis=1).astype(BF16)
    pad = HEAD_PAD - QK_DIM
    hw = MLA_HEADS * HEAD_PAD
    wq = w_uq.reshape(Q_LORA_RANK, MLA_HEADS, QK_DIM)
    nope, r1, r2 = wq[..., :QK_NOPE_DIM], wq[..., QK_NOPE_DIM:QK_NOPE_DIM + ROPE_HALF], wq[..., QK_NOPE_DIM + ROPE_HALF:]
    zpad = jnp.zeros((Q_LORA_RANK, MLA_HEADS, pad), F32)
    wqa_t = jnp.concatenate([nope, r1, r2, zpad], axis=-1).reshape(Q_LORA_RANK, hw).T.astype(BF16)
    wqb_t = jnp.concatenate([jnp.zeros_like(nope), r2, r1, zpad], axis=-1).reshape(Q_LORA_RANK, hw).T.astype(BF16)
    wkv = w_ukv.reshape(KV_LORA_RANK, MLA_HEADS, QK_NOPE_DIM + V_HEAD_DIM)
    wk = jnp.concatenate([wkv[..., :QK_NOPE_DIM], jnp.zeros((KV_LORA_RANK, MLA_HEADS, HEAD_PAD - QK_NOPE_DIM), F32)],
                         axis=-1).reshape(KV_LORA_RANK, hw).astype(BF16)
    wv_t = jnp.concatenate([wkv[..., QK_NOPE_DIM:], jnp.zeros((KV_LORA_RANK, MLA_HEADS, V_ROWS - V_HEAD_DIM), F32)],
                           axis=-1).reshape(KV_LORA_RANK, MLA_HEADS * V_ROWS).T.astype(BF16)
    vones = jnp.zeros((MLA_HEADS, V_ROWS), F32).at[:, V_HEAD_DIM].set(1.0).reshape(MLA_HEADS * V_ROWS, 1)
    wbm = jnp.concatenate(
        [w_br_mla.reshape(MLA_HEADS, V_HEAD_DIM, D_MODEL),
         jnp.zeros((MLA_HEADS, HEAD_PAD - V_HEAD_DIM, D_MODEL), F32)], axis=1).astype(BF16)
    n_r = N_GROUPS + N_EXPERTS
    wr = jnp.concatenate([w_rg, w_re, jnp.zeros((D_MODEL, LANES - n_r), F32)], axis=1)
    wr_hi = wr.astype(BF16)
    wr_lo = (wr - wr_hi.astype(F32)).astype(BF16)
    rw = jnp.concatenate([wr_hi, wr_lo], axis=1)
    rb = jnp.concatenate([b_rg, b_re, jnp.zeros((LANES - n_r,), F32)]).reshape(1, LANES)
    bs_full = jnp.repeat(b_s.T, SGU_GROUP_DIM, axis=1)
    fcol = (ROPE_THETA ** (-jnp.arange(ROPE_HALF, dtype=F32) / ROPE_HALF)).reshape(ROPE_HALF, 1)
    return w_in_p, wqa_t, wqb_t, wk, wv_t, vones, wbm, rw, rb, bs_full, fcol


def kernel(x, c, positions, w_ada, b_ada, norm1_g, w_in, q_norm_g, w_uq, kv_norm_g, w_ukv, v_norm_g, v_norm_b, w_s, b_s, w_br_mla, w_br_sgu, w_out, norm2_g, w_rg, b_rg, w_re, b_re, w1, w3, w2, final_g):
    batch, seq, _ = x.shape
    n_tok = batch * seq
    assert w_ada.shape[0] == 1, "single-layer block"
    l = 0
    row = lambda v: v.reshape(1, -1)
    (w_in_p, wqa_t, wqb_t, wk, wv_t, vones, wbm, rw, rb, bs_full, fcol) = _pack_weights(
        w_in[l], w_uq[l], w_ukv[l], w_br_mla[l], w_rg[l], b_rg[l], w_re[l], b_re[l], b_s[l])
    mod = _adaln(c, w_ada[l], b_ada[l]).reshape(batch, N_MOD, D_MODEL)
    qt, k, vt, gm, ms = _pre(
        x, mod, positions, row(norm1_g[l]), w_in_p, row(q_norm_g[l]), wqa_t, wqb_t, row(kv_norm_g[l]),
        wk, wv_t, vones, fcol, row(v_norm_g[l]), row(v_norm_b[l]), w_s[l], bs_full,
        w_br_sgu[l].astype(BF16))
    o = _flash(qt, k, vt)
    x2, h2, rt, rwt, cnt = _post(x, o, gm, ms, mod, wbm, w_out[l].astype(BF16), row(norm2_g[l]), rw, rb)

    counts = cnt[0, :N_EXPERTS].astype(jnp.int32)
    padded = ((counts + EXPERT_BLOCK - 1) // EXPERT_BLOCK) * EXPERT_BLOCK
    pad_ends = jnp.cumsum(padded)
    pad_starts = pad_ends - padded
    n_assign = n_tok * TOP_K
    rows_total = ((n_assign + EXPERT_BLOCK - 1) // EXPERT_BLOCK + N_EXPERTS) * EXPERT_BLOCK
    nblk = rows_total // EXPERT_BLOCK
    rti = rt.astype(jnp.int32)
    expert_ids = jnp.arange(N_EXPERTS, dtype=jnp.int32)
    dests = []
    for kk in range(TOP_K):
        e_k = rti[:, :, kk, :].reshape(n_tok)
        r_k = rti[:, :, TOP_K + kk, :].reshape(n_tok)
        start_k = jnp.sum(jnp.where(e_k[:, None] == expert_ids[None, :], pad_starts[None, :], 0), axis=1)
        dests.append(start_k + r_k)
    tok_ids = jnp.arange(n_tok, dtype=jnp.int32)
    slot_tok = (jnp.arange(rows_total, dtype=jnp.int32) % n_tok).at[jnp.concatenate(dests)].set(
        jnp.concatenate([tok_ids] * TOP_K), unique_indices=True, mode='promise_in_bounds')
    blk_start = jnp.arange(nblk, dtype=jnp.int32) * EXPERT_BLOCK
    blk_e = jnp.minimum(jnp.sum((pad_ends[None, :] <= blk_start[:, None]).astype(jnp.int32), axis=1),
                        N_EXPERTS - 1)
    n_real = pad_ends[-1:] // EXPERT_BLOCK
    x_sorted = h2.reshape(n_tok, D_MODEL)[slot_tok]
    y = _experts(blk_e, n_real, x_sorted, w1, w3, w2)
    y0, y1 = [y[d].reshape(batch, seq, D_MODEL) for d in dests]
    return _final(x2, y0, y1, rwt, mod, row(final_g))
```

```python
import jax
import jax.numpy as jnp
from jax import lax
from jax.experimental import pallas as pl
from jax.experimental.pallas import tpu as pltpu

F32 = jnp.float32
BF16 = jnp.bfloat16

D_MODEL = 1024
MLA_HEADS = 8
QK_NOPE_DIM = 64
QK_ROPE_DIM = 32
ROPE_HALF = QK_ROPE_DIM // 2
QK_DIM = QK_NOPE_DIM + QK_ROPE_DIM
V_HEAD_DIM = 64
Q_LORA_RANK = 256
KV_LORA_RANK = 128
ROPE_THETA = 10000.0
SGU_GROUPS = 8
SGU_GROUP_DIM = 64
SGU_WIDTH = SGU_GROUPS * SGU_GROUP_DIM
SGU_CHUNK = 128
N_GROUPS = 4
EXPERTS_PER_GROUP = 8
N_EXPERTS = N_GROUPS * EXPERTS_PER_GROUP
TOP_K = 2
EXPERT_FF = 512
N_MOD = 6
EPS = 1e-6

LANES = 128
HEAD_PAD = LANES
BF16_SUBLANES = 16
V_ROWS = -(-(V_HEAD_DIM + 1) // BF16_SUBLANES) * BF16_SUBLANES
VMEM_LIMIT = 48 * 1024 * 1024

TOKEN_TILE = 512
ATTN_TILE = 512
ATTN_CHAINS = 4
EXPERT_BLOCK = 512
ADALN_TILE = 512
ROUTE_ROWS = 8

_SEG_CQ = (0, 256)
_SEG_CKV = (256, 384)
_SEG_KR = (384, 512)
_SEG_U = (512, 1024)
_SEG_V = (1024, 1536)
_SEG_GMLA = (1536, 2560)
_SEG_GSGU = (2560, 3584)
PACKED_IN = 3584

Q_SCALE = (QK_DIM ** -0.5) * 1.4426950408889634

_NT_DIMS = (((1,), (1,)), ((), ()))


def _sigmoid(x):
    return 1.0 / (1.0 + jnp.exp(-x))


def _gelu_tanh(x):
    c = 0.7978845608028654
    return 0.5 * x * (1.0 + jnp.tanh(c * (x + 0.044715 * (x * x * x))))


def _rms(x, g):
    return x * lax.rsqrt(jnp.mean(x * x, axis=-1, keepdims=True) + EPS) * g


def _dot(a, b):
    return jnp.dot(a, b, preferred_element_type=F32)


def _dot_nt(a, b):
    return lax.dot_general(a, b, _NT_DIMS, preferred_element_type=F32)


def _adaln_kernel(ct_ref, w_ref, b_ref, o_ref):
    ct = ct_ref[...]
    s = ct * _sigmoid(ct)
    w = w_ref[...]
    rows = [jnp.sum(s[:, b:b + 1] * w, axis=0, keepdims=True) for b in range(ct.shape[1])]
    o_ref[...] = jnp.concatenate(rows, axis=0) + b_ref[...]


def _adaln(c, w_ada, b_ada):
    batch = c.shape[0]
    width = w_ada.shape[1]
    return pl.pallas_call(
        _adaln_kernel,
        grid=(width // ADALN_TILE,),
        in_specs=[
            pl.BlockSpec((D_MODEL, batch), lambda j: (0, 0)),
            pl.BlockSpec((D_MODEL, ADALN_TILE), lambda j: (0, j)),
            pl.BlockSpec((1, ADALN_TILE), lambda j: (0, j)),
        ],
        out_specs=pl.BlockSpec((batch, ADALN_TILE), lambda j: (0, j)),
        out_shape=jax.ShapeDtypeStruct((batch, width), F32),
        compiler_params=pltpu.CompilerParams(dimension_semantics=("arbitrary",)),
        name="adaln",
    )(c.T, w_ada, b_ada.reshape(1, width))


def _pre_kernel(x_ref, mod_ref, pos_ref, g1_ref, win_ref, qg_ref, wqa_ref, wqb_ref,
                kvg_ref, wk_ref, wv_ref, vones_ref, fcol_ref,
                vng_ref, vnb_ref, ws_ref, bs_ref, wbs_ref,
                qt_out, k_out, vt_out, gm_out, ms_out):
    tm = x_ref.shape[1]
    x = x_ref[0]
    shift1 = mod_ref[0, 0:1, :]
    scale1 = mod_ref[0, 1:2, :]
    h1 = _rms(x, g1_ref[...]) * (1.0 + scale1) + shift1
    hb = h1.astype(BF16)

    def proj(seg):
        return _dot(hb, win_ref[:, seg[0]:seg[1]])

    pos = pos_ref[0].astype(F32)
    ang = fcol_ref[...] * pos
    cos_t = jnp.cos(ang)
    sin_t = jnp.sin(ang)
    pad_rows = HEAD_PAD - QK_DIM

    cqn = _rms(proj(_SEG_CQ), qg_ref[...]).astype(BF16)
    qa_t = _dot_nt(wqa_ref[...], cqn)
    qb_t = _dot_nt(wqb_ref[...], cqn)
    c_tab = jnp.concatenate([jnp.ones((QK_NOPE_DIM, tm), F32), cos_t, cos_t,
                             jnp.ones((pad_rows, tm), F32)], axis=0) * Q_SCALE
    s_tab = jnp.concatenate([jnp.zeros((QK_NOPE_DIM, tm), F32), -sin_t, sin_t,
                             jnp.zeros((pad_rows, tm), F32)], axis=0) * Q_SCALE
    for h in range(MLA_HEADS):
        sl = slice(h * HEAD_PAD, (h + 1) * HEAD_PAD)
        qt_out[0, h, 0] = (qa_t[sl] * c_tab + qb_t[sl] * s_tab).astype(BF16)

    ckvn = _rms(proj(_SEG_CKV), kvg_ref[...]).astype(BF16)
    kr_t = proj(_SEG_KR).T
    x1 = kr_t[0:ROPE_HALF]
    x2 = kr_t[ROPE_HALF:QK_ROPE_DIM]
    k_pe = jnp.concatenate([jnp.zeros((QK_NOPE_DIM, tm), F32), x1 * cos_t - x2 * sin_t,
                            x1 * sin_t + x2 * cos_t, jnp.zeros((pad_rows, tm), F32)], axis=0).T
    k_full = _dot(ckvn, wk_ref[...])
    v_t = _dot_nt(wv_ref[...], ckvn) + vones_ref[...]
    for h in range(MLA_HEADS):
        sl = slice(h * HEAD_PAD, (h + 1) * HEAD_PAD)
        k_out[0, h] = (k_full[:, sl] + k_pe).astype(BF16)
        vt_out[0, h, 0] = v_t[h * V_ROWS:(h + 1) * V_ROWS].astype(BF16)

    gu = _gelu_tanh(proj(_SEG_U))
    gv = _gelu_tanh(proj(_SEG_V))
    mu = jnp.mean(gv, axis=-1, keepdims=True)
    dv = gv - mu
    var = jnp.mean(dv * dv, axis=-1, keepdims=True)
    vln = (dv * lax.rsqrt(var + EPS) * vng_ref[...] + vnb_ref[...]).astype(BF16)
    row = lax.broadcasted_iota(jnp.int32, (SGU_CHUNK, SGU_CHUNK), 0)
    col = lax.broadcasted_iota(jnp.int32, (SGU_CHUNK, SGU_CHUNK), 1)
    causal = col <= row
    w_tril = [jnp.where(causal, ws_ref[g], 0.0).astype(BF16) for g in range(SGU_GROUPS)]
    lane_grp = lax.broadcasted_iota(jnp.int32, (SGU_CHUNK, SGU_WIDTH), 1) // SGU_GROUP_DIM
    chunks = []
    for c in range(tm // SGU_CHUNK):
        vc = vln[c * SGU_CHUNK:(c + 1) * SGU_CHUNK]
        acc = jnp.zeros((SGU_CHUNK, SGU_WIDTH), F32)
        for g in range(SGU_GROUPS):
            acc = jnp.where(lane_grp == g, _dot(w_tril[g], vc), acc)
        chunks.append(acc + bs_ref[...])
    sgu = (gu * jnp.concatenate(chunks, axis=0)).astype(BF16)
    y_sgu = _dot(sgu, wbs_ref[...])
    ms_out[0] = (_sigmoid(proj(_SEG_GSGU)) * y_sgu).astype(BF16)
    gm_out[0] = _sigmoid(proj(_SEG_GMLA)).astype(BF16)


def _pre(x, mod, positions, g1, w_in_p, qg, wqa_t, wqb_t, kvg, wk, wv_t, vones, fcol,
         vng, vnb, w_s, bs_full, wbs):
    batch, seq, _ = x.shape
    tm = TOKEN_TILE
    assert tm == ATTN_TILE and seq % tm == 0
    nt = seq // tm
    hw = MLA_HEADS * HEAD_PAD

    def const(shape):
        return pl.BlockSpec(shape, lambda b, i: (0,) * len(shape))

    return pl.pallas_call(
        _pre_kernel,
        grid=(batch, nt),
        in_specs=[
            pl.BlockSpec((1, tm, D_MODEL), lambda b, i: (b, i, 0)),
            pl.BlockSpec((1, N_MOD, D_MODEL), lambda b, i: (b, 0, 0)),
            pl.BlockSpec((1, 1, tm), lambda b, i: (b, 0, i)),
            const((1, D_MODEL)),
            const((D_MODEL, PACKED_IN)),
            const((1, Q_LORA_RANK)),
            const((hw, Q_LORA_RANK)),
            const((hw, Q_LORA_RANK)),
            const((1, KV_LORA_RANK)),
            const((KV_LORA_RANK, hw)),
            const((MLA_HEADS * V_ROWS, KV_LORA_RANK)),
            const((MLA_HEADS * V_ROWS, 1)),
            const((ROPE_HALF, 1)),
            const((1, SGU_WIDTH)),
            const((1, SGU_WIDTH)),
            const((SGU_GROUPS, SGU_CHUNK, SGU_CHUNK)),
            const((SGU_CHUNK, SGU_WIDTH)),
            const((SGU_WIDTH, D_MODEL)),
        ],
        out_specs=[
            pl.BlockSpec((1, MLA_HEADS, 1, HEAD_PAD, tm), lambda b, i: (b, 0, i, 0, 0)),
            pl.BlockSpec((1, MLA_HEADS, tm, HEAD_PAD), lambda b, i: (b, 0, i, 0)),
            pl.BlockSpec((1, MLA_HEADS, 1, V_ROWS, tm), lambda b, i: (b, 0, i, 0, 0)),
            pl.BlockSpec((1, tm, D_MODEL), lambda b, i: (b, i, 0)),
            pl.BlockSpec((1, tm, D_MODEL), lambda b, i: (b, i, 0)),
        ],
        out_shape=[
            jax.ShapeDtypeStruct((batch, MLA_HEADS, nt, HEAD_PAD, tm), BF16),
            jax.ShapeDtypeStruct((batch, MLA_HEADS, seq, HEAD_PAD), BF16),
            jax.ShapeDtypeStruct((batch, MLA_HEADS, nt, V_ROWS, tm), BF16),
            jax.ShapeDtypeStruct((batch, seq, D_MODEL), BF16),
            jax.ShapeDtypeStruct((batch, seq, D_MODEL), BF16),
        ],
        compiler_params=pltpu.CompilerParams(
            dimension_semantics=("arbitrary", "arbitrary"), vmem_limit_bytes=VMEM_LIMIT),
        name="pre",
    )(x, mod, positions.reshape(batch, 1, seq), g1, w_in_p, qg, wqa_t, wqb_t, kvg, wk, wv_t, vones,
      fcol, vng, vnb, w_s, bs_full, wbs)


def _flash_kernel(qt_ref, k_ref, vt_ref, o_ref, m_ref, acc_ref, s0_ref, s1_ref):
    t = qt_ref.shape[4]
    i = pl.program_id(2)
    m_ref[...] = jnp.full(m_ref.shape, -1e30, F32)
    acc_ref[...] = jnp.zeros(acc_ref.shape, F32)

    def logits(c, j):
        start = pl.multiple_of(j * t, t)
        return _dot(k_ref[0, 0, pl.ds(start, t), :], qt_ref[0, 0, c])

    def update(c, j, s_ref, masked):
        def read():
            s = s_ref[c]
            if masked:
                key = lax.broadcasted_iota(jnp.int32, (t, t), 0)
                qry = lax.broadcasted_iota(jnp.int32, (t, t), 1)
                s = jnp.where(key <= qry, s, -jnp.inf)
            return s
        m_old = m_ref[c]
        m_new = jnp.maximum(m_old, jnp.max(read(), axis=0, keepdims=True))
        p = jnp.exp2(read() - m_new).astype(BF16)
        pv = _dot(vt_ref[0, 0, j], p)
        acc_ref[c] = jnp.exp2(m_old - m_new) * acc_ref[c] + pv
        m_ref[c] = m_new

    bufs = (s0_ref, s1_ref)
    chains = list(range(ATTN_CHAINS))
    last = ATTN_CHAINS - 1
    first = ATTN_CHAINS * i

    for c in chains:
        s0_ref[c] = logits(c, 0)
    for c in chains[:last]:
        s1_ref[c] = logits(c, 1)

    def body(u, carry):
        base = ATTN_CHAINS * u
        s1_ref[last] = logits(last, base + 1)
        for kk in range(ATTN_CHAINS):
            for c in chains:
                update(c, base + kk, bufs[kk % 2], False)
                if not (kk == ATTN_CHAINS - 1 and c == last):
                    bufs[kk % 2][c] = logits(c, base + kk + 2)
        return carry

    lax.fori_loop(0, i, body, 0)
    s1_ref[last] = logits(last, first + 1)
    for d in range(ATTN_CHAINS):
        for c in chains[d:]:
            update(c, first + d, bufs[d % 2], c == d)
            if c >= d + 2:
                bufs[d % 2][c] = logits(c, first + d + 2)
    for c in range(ATTN_CHAINS):
        acc = acc_ref[c]
        o_ref[0, 0, c] = (acc[0:V_HEAD_DIM] / acc[V_HEAD_DIM:V_HEAD_DIM + 1, :]).astype(BF16)


def _flash(qt, k, vt):
    batch, heads, nt, _, t = qt.shape
    seq = nt * t
    assert nt % ATTN_CHAINS == 0 and ATTN_CHAINS % 2 == 0
    return pl.pallas_call(
        _flash_kernel,
        grid=(batch, heads, nt // ATTN_CHAINS),
        in_specs=[
            pl.BlockSpec((1, 1, ATTN_CHAINS, HEAD_PAD, t), lambda b, h, i: (b, h, i, 0, 0)),
            pl.BlockSpec((1, 1, seq, HEAD_PAD), lambda b, h, i: (b, h, 0, 0)),
            pl.BlockSpec((1, 1, nt, V_ROWS, t), lambda b, h, i: (b, h, 0, 0, 0)),
        ],
        out_specs=pl.BlockSpec((1, 1, ATTN_CHAINS, V_HEAD_DIM, t), lambda b, h, i: (b, h, i, 0, 0)),
        out_shape=jax.ShapeDtypeStruct((batch, heads, nt, V_HEAD_DIM, t), BF16),
        scratch_shapes=[pltpu.VMEM((ATTN_CHAINS, 1, t), F32), pltpu.VMEM((ATTN_CHAINS, V_ROWS, t), F32),
                        pltpu.VMEM((ATTN_CHAINS, t, t), F32), pltpu.VMEM((ATTN_CHAINS, t, t), F32)],
        compiler_params=pltpu.CompilerParams(
            dimension_semantics=("arbitrary", "arbitrary", "arbitrary"),
            vmem_limit_bytes=VMEM_LIMIT),
        name="flash",
    )(qt, k, vt)


def _post_kernel(x_ref, o_ref, gm_ref, ms_ref, mod_ref, wbm_ref, wout_ref, g2_ref, rw_ref, rb_ref,
                 x2_out, h2_out, rt_out, rwt_out, cnt_out, carry_ref):
    tm = x_ref.shape[1]

    @pl.when((pl.program_id(0) == 0) & (pl.program_id(1) == 0))
    def _():
        carry_ref[...] = jnp.zeros(carry_ref.shape, F32)

    gate1 = mod_ref[0, 2:3, :]
    shift2 = mod_ref[0, 3:4, :]
    scale2 = mod_ref[0, 4:5, :]
    o_t = jnp.concatenate([o_ref[0, h, 0] for h in range(MLA_HEADS)], axis=0)
    y_mla = lax.dot_general(o_t, wbm_ref[...], (((0,), (0,)), ((), ())), preferred_element_type=F32)
    merged = gm_ref[0].astype(F32) * y_mla + ms_ref[0].astype(F32)
    x2 = x_ref[0] + gate1 * _dot(merged.astype(BF16), wout_ref[...])
    x2_out[0] = x2
    h2 = _rms(x2, g2_ref[...]) * (1.0 + scale2) + shift2
    h2_out[0] = h2.astype(BF16)

    hi = h2.astype(BF16)
    lo = (h2 - hi.astype(F32)).astype(BF16)
    a = _dot(hi, rw_ref[...])
    b = _dot(lo, rw_ref[:, 0:LANES])
    logits = a[:, 0:LANES] + a[:, LANES:2 * LANES] + b + rb_ref[...]

    lane = lax.broadcasted_iota(jnp.int32, (tm, LANES), 1).astype(F32)
    big = float(LANES)
    ninf = -jnp.inf
    gl = jnp.where(lane < N_GROUPS, logits, ninf)
    gmax = jnp.max(gl, axis=-1, keepdims=True)
    grp = jnp.min(jnp.where(gl == gmax, lane, big), axis=-1, keepdims=True)
    p_grp = 1.0 / jnp.sum(jnp.exp(gl - gmax), axis=-1, keepdims=True)
    first = N_GROUPS + EXPERTS_PER_GROUP * grp
    el = jnp.where((lane >= first) & (lane < first + EXPERTS_PER_GROUP), logits, ninf)
    m1 = jnp.max(el, axis=-1, keepdims=True)
    i1 = jnp.min(jnp.where(el == m1, lane, big), axis=-1, keepdims=True)
    el2 = jnp.where(lane == i1, ninf, el)
    m2 = jnp.max(el2, axis=-1, keepdims=True)
    i2 = jnp.min(jnp.where(el2 == m2, lane, big), axis=-1, keepdims=True)
    tt = jnp.exp(m2 - m1)
    w0 = p_grp / (1.0 + tt)
    w1 = p_grp * tt / (1.0 + tt)
    e0 = i1 - N_GROUPS
    e1 = i2 - N_GROUPS

    onehot = jnp.where((lane == e0) | (lane == e1), 1.0, 0.0)
    row = lax.broadcasted_iota(jnp.int32, (tm, tm), 0)
    col = lax.broadcasted_iota(jnp.int32, (tm, tm), 1)
    lower = jnp.where(col < row, 1.0, 0.0).astype(BF16)
    before = _dot(lower, onehot.astype(BF16)) + carry_ref[...]
    r0 = jnp.sum(jnp.where(lane == e0, before, 0.0), axis=-1, keepdims=True)
    r1 = jnp.sum(jnp.where(lane == e1, before, 0.0), axis=-1, keepdims=True)
    carry_ref[...] = carry_ref[...] + jnp.sum(onehot, axis=0, keepdims=True)
    cnt_out[...] = carry_ref[...]

    ri = jnp.where(lane == 0, e0, jnp.where(lane == 1, e1, jnp.where(lane == 2, r0, jnp.where(lane == 3, r1, 0.0))))
    rt_out[0, 0] = ri.T[0:ROUTE_ROWS]
    rwt_out[0] = jnp.where(lane == 0, w0, jnp.where(lane == 1, w1, 0.0))


def _post(x, o, gm, ms, mod, wbm, wout, g2, rw, rb):
    batch, seq, _ = x.shape
    tm = TOKEN_TILE

    def const(shape):
        return pl.BlockSpec(shape, lambda b, i: (0,) * len(shape))

    tok = lambda w: pl.BlockSpec((1, tm, w), lambda b, i: (b, i, 0))
    return pl.pallas_call(
        _post_kernel,
        grid=(batch, seq // tm),
        in_specs=[
            tok(D_MODEL),
            pl.BlockSpec((1, MLA_HEADS, 1, V_HEAD_DIM, tm), lambda b, i: (b, 0, i, 0, 0)),
            tok(D_MODEL),
            tok(D_MODEL),
            pl.BlockSpec((1, N_MOD, D_MODEL), lambda b, i: (b, 0, 0)),
            const((MLA_HEADS * V_HEAD_DIM, D_MODEL)),
            const((D_MODEL, D_MODEL)),
            const((1, D_MODEL)),
            const((D_MODEL, 2 * LANES)),
            const((1, LANES)),
        ],
        out_specs=[tok(D_MODEL), tok(D_MODEL),
                   pl.BlockSpec((1, 1, ROUTE_ROWS, tm), lambda b, i: (b, i, 0, 0)),
                   tok(LANES), const((1, LANES))],
        out_shape=[
            jax.ShapeDtypeStruct((batch, seq, D_MODEL), F32),
            jax.ShapeDtypeStruct((batch, seq, D_MODEL), BF16),
            jax.ShapeDtypeStruct((batch, seq // tm, ROUTE_ROWS, tm), F32),
            jax.ShapeDtypeStruct((batch, seq, LANES), F32),
            jax.ShapeDtypeStruct((1, LANES), F32),
        ],
        scratch_shapes=[pltpu.VMEM((1, LANES), F32)],
        compiler_params=pltpu.CompilerParams(
            dimension_semantics=("arbitrary", "arbitrary"), vmem_limit_bytes=VMEM_LIMIT),
        name="post",
    )(x, o, gm, ms, mod, wbm, wout, g2, rw, rb)


def _expert_kernel(be_ref, nr_ref, x_ref, w1_ref, w3_ref, w2_ref, y_ref):
    j = pl.program_id(0)

    @pl.when(j < nr_ref[0])
    def _():
        x = x_ref[...]
        a = _dot(x, w1_ref[0, 0].astype(BF16))
        b = _dot(x, w3_ref[0, 0].astype(BF16))
        act = (a * _sigmoid(a) * b).astype(BF16)
        y_ref[...] = _dot(act, w2_ref[0, 0].astype(BF16)).astype(BF16)

    @pl.when(j >= nr_ref[0])
    def _():
        y_ref[...] = jnp.zeros(y_ref.shape, BF16)


def _experts(blk_e, n_real, x_sorted, w1, w3, w2):
    rows = x_sorted.shape[0]
    nblk = rows // EXPERT_BLOCK
    last = lambda j, be, nr: jnp.minimum(j, nr[0] - 1)
    w_map = lambda j, be, nr: (0, be[last(j, be, nr)], 0, 0)
    return pl.pallas_call(
        _expert_kernel,
        grid_spec=pltpu.PrefetchScalarGridSpec(
            num_scalar_prefetch=2,
            grid=(nblk,),
            in_specs=[
                pl.BlockSpec((EXPERT_BLOCK, D_MODEL), lambda j, be, nr: (last(j, be, nr), 0)),
                pl.BlockSpec((1, 1, D_MODEL, EXPERT_FF), w_map),
                pl.BlockSpec((1, 1, D_MODEL, EXPERT_FF), w_map),
                pl.BlockSpec((1, 1, EXPERT_FF, D_MODEL), w_map),
            ],
            out_specs=pl.BlockSpec((EXPERT_BLOCK, D_MODEL), lambda j, be, nr: (j, 0)),
        ),
        out_shape=jax.ShapeDtypeStruct((rows, D_MODEL), BF16),
        compiler_params=pltpu.CompilerParams(
            dimension_semantics=("arbitrary",), vmem_limit_bytes=VMEM_LIMIT),
        name="experts",
    )(blk_e, n_real, x_sorted, w1, w3, w2)


def _final_kernel(x2_ref, y0_ref, y1_ref, rwt_ref, mod_ref, fg_ref, out_ref):
    gate2 = mod_ref[0, 5:6, :]
    rwt = rwt_ref[0]
    moe = rwt[:, 0:1] * y0_ref[0].astype(F32) + rwt[:, 1:2] * y1_ref[0].astype(F32)
    x3 = x2_ref[0] + gate2 * moe
    out_ref[0] = _rms(x3, fg_ref[...])


def _final(x2, y0, y1, rwt, mod, fg):
    batch, seq, _ = x2.shape
    tm = TOKEN_TILE
    tok = lambda w: pl.BlockSpec((1, tm, w), lambda b, i: (b, i, 0))
    return pl.pallas_call(
        _final_kernel,
        grid=(batch, seq // tm),
        in_specs=[
            tok(D_MODEL), tok(D_MODEL), tok(D_MODEL), tok(LANES),
            pl.BlockSpec((1, N_MOD, D_MODEL), lambda b, i: (b, 0, 0)),
            pl.BlockSpec((1, D_MODEL), lambda b, i: (0, 0)),
        ],
        out_specs=tok(D_MODEL),
        out_shape=jax.ShapeDtypeStruct((batch, seq, D_MODEL), F32),
        compiler_params=pltpu.CompilerParams(
            dimension_semantics=("arbitrary", "arbitrary"), vmem_limit_bytes=VMEM_LIMIT),
        name="final",
    )(x2, y0, y1, rwt, mod, fg)


def _pack_weights(w_in, w_uq, w_ukv, w_br_mla, w_rg, b_rg, w_re, b_re, b_s):
    kr_end = Q_LORA_RANK + KV_LORA_RANK + QK_ROPE_DIM
    w_in_p = jnp.concatenate(
        [w_in[:, :kr_end], jnp.zeros((D_MODEL, _SEG_KR[1] - kr_end), F32), w_in[:, kr_end:]],
        axis=1).astype(BF16)
    pad = HEAD_PAD - QK_DIM
    hw = MLA_HEADS * HEAD_PAD
    wq = w_uq.reshape(Q_LORA_RANK, MLA_HEADS, QK_DIM)
    nope, r1, r2 = wq[..., :QK_NOPE_DIM], wq[..., QK_NOPE_DIM:QK_NOPE_DIM + ROPE_HALF], wq[..., QK_NOPE_DIM + ROPE_HALF:]
    zpad = jnp.zeros((Q_LORA_RANK, MLA_HEADS, pad), F32)
    wqa_t = jnp.concatenate([nope, r1, r2, zpad], axis=-1).reshape(Q_LORA_RANK, hw).T.astype(BF16)
    wqb_t = jnp.concatenate([jnp.zeros_like(nope), r2, r1, zpad], axis=-1).reshape(Q_LORA_RANK, hw).T.astype(BF16)
    wkv = w_ukv.reshape(KV_LORA_RANK, MLA_HEADS, QK_NOPE_DIM + V_HEAD_DIM)
    wk = jnp.concatenate([wkv[..., :QK_NOPE_DIM], jnp.zeros((KV_LORA_RANK, MLA_HEADS, HEAD_PAD - QK_NOPE_DIM), F32)],
                         axis=-1).reshape(KV_LORA_RANK, hw).astype(BF16)
    wv_t = jnp.concatenate([wkv[..., QK_NOPE_DIM:], jnp.zeros((KV_LORA_RANK, MLA_HEADS, V_ROWS - V_HEAD_DIM), F32)],
                           axis=-1).reshape(KV_LORA_RANK, MLA_HEADS * V_ROWS).T.astype(BF16)
    vones = jnp.zeros((MLA_HEADS, V_ROWS), F32).at[:, V_HEAD_DIM].set(1.0).reshape(MLA_HEADS * V_ROWS, 1)
    wbm = w_br_mla.astype(BF16)
    n_r = N_GROUPS + N_EXPERTS
    wr = jnp.concatenate([w_rg, w_re, jnp.zeros((D_MODEL, LANES - n_r), F32)], axis=1)
    wr_hi = wr.astype(BF16)
    wr_lo = (wr - wr_hi.astype(F32)).astype(BF16)
    rw = jnp.concatenate([wr_hi, wr_lo], axis=1)
    rb = jnp.concatenate([b_rg, b_re, jnp.zeros((LANES - n_r,), F32)]).reshape(1, LANES)
    bs_full = jnp.repeat(b_s.T, SGU_GROUP_DIM, axis=1)
    fcol = (ROPE_THETA ** (-jnp.arange(ROPE_HALF, dtype=F32) / ROPE_HALF)).reshape(ROPE_HALF, 1)
    return w_in_p, wqa_t, wqb_t, wk, wv_t, vones, wbm, rw, rb, bs_full, fcol


def kernel(x, c, positions, w_ada, b_ada, norm1_g, w_in, q_norm_g, w_uq, kv_norm_g, w_ukv, v_norm_g, v_norm_b, w_s, b_s, w_br_mla, w_br_sgu, w_out, norm2_g, w_rg, b_rg, w_re, b_re, w1, w3, w2, final_g):
    batch, seq, _ = x.shape
    n_tok = batch * seq
    assert w_ada.shape[0] == 1, "single-layer block"
    l = 0
    row = lambda v: v.reshape(1, -1)
    (w_in_p, wqa_t, wqb_t, wk, wv_t, vones, wbm, rw, rb, bs_full, fcol) = _pack_weights(
        w_in[l], w_uq[l], w_ukv[l], w_br_mla[l], w_rg[l], b_rg[l], w_re[l], b_re[l], b_s[l])
    mod = _adaln(c, w_ada[l], b_ada[l]).reshape(batch, N_MOD, D_MODEL)
    qt, k, vt, gm, ms = _pre(
        x, mod, positions, row(norm1_g[l]), w_in_p, row(q_norm_g[l]), wqa_t, wqb_t, row(kv_norm_g[l]),
        wk, wv_t, vones, fcol, row(v_norm_g[l]), row(v_norm_b[l]), w_s[l], bs_full,
        w_br_sgu[l].astype(BF16))
    o = _flash(qt, k, vt)
    x2, h2, rt, rwt, cnt = _post(x, o, gm, ms, mod, wbm, w_out[l].astype(BF16), row(norm2_g[l]), rw, rb)

    counts = cnt[0, :N_EXPERTS].astype(jnp.int32)
    padded = ((counts + EXPERT_BLOCK - 1) // EXPERT_BLOCK) * EXPERT_BLOCK
    pad_ends = jnp.cumsum(padded)
    pad_starts = pad_ends - padded
    n_assign = n_tok * TOP_K
    rows_total = ((n_assign + EXPERT_BLOCK - 1) // EXPERT_BLOCK + N_EXPERTS) * EXPERT_BLOCK
    nblk = rows_total // EXPERT_BLOCK
    rti = rt.astype(jnp.int32)
    expert_ids = jnp.arange(N_EXPERTS, dtype=jnp.int32)
    dests = []
    for kk in range(TOP_K):
        e_k = rti[:, :, kk, :].reshape(n_tok)
        r_k = rti[:, :, TOP_K + kk, :].reshape(n_tok)
        start_k = jnp.sum(jnp.where(e_k[:, None] == expert_ids[None, :], pad_starts[None, :], 0), axis=1)
        dests.append(start_k + r_k)
    tok_ids = jnp.arange(n_tok, dtype=jnp.int32)
    slot_tok = (jnp.arange(rows_total, dtype=jnp.int32) % n_tok).at[jnp.concatenate(dests)].set(
        jnp.concatenate([tok_ids] * TOP_K), unique_indices=True, mode='promise_in_bounds')
    blk_start = jnp.arange(nblk, dtype=jnp.int32) * EXPERT_BLOCK
    blk_e = jnp.minimum(jnp.sum((pad_ends[None, :] <= blk_start[:, None]).astype(jnp.int32), axis=1),
                        N_EXPERTS - 1)
    n_real = pad_ends[-1:] // EXPERT_BLOCK
    x_sorted = h2.reshape(n_tok, D_MODEL)[slot_tok]
    y = _experts(blk_e, n_real, x_sorted, w1, w3, w2)
    y0, y1 = [y[d].reshape(batch, seq, D_MODEL) for d in dests]
    return _final(x2, y0, y1, rwt, mod, row(final_g))
```

```python
import jax
import jax.numpy as jnp
from jax import lax
from jax.experimental import pallas as pl
from jax.experimental.pallas import tpu as pltpu

F32 = jnp.float32
BF16 = jnp.bfloat16

D_MODEL = 1024
MLA_HEADS = 8
QK_NOPE_DIM = 64
QK_ROPE_DIM = 32
ROPE_HALF = QK_ROPE_DIM // 2
QK_DIM = QK_NOPE_DIM + QK_ROPE_DIM
V_HEAD_DIM = 64
Q_LORA_RANK = 256
KV_LORA_RANK = 128
ROPE_THETA = 10000.0
SGU_GROUPS = 8
SGU_GROUP_DIM = 64
SGU_WIDTH = SGU_GROUPS * SGU_GROUP_DIM
SGU_CHUNK = 128
N_GROUPS = 4
EXPERTS_PER_GROUP = 8
N_EXPERTS = N_GROUPS * EXPERTS_PER_GROUP
TOP_K = 2
EXPERT_FF = 512
N_MOD = 6
EPS = 1e-6

LANES = 128
HEAD_PAD = LANES
BF16_SUBLANES = 16
V_ROWS = -(-(V_HEAD_DIM + 1) // BF16_SUBLANES) * BF16_SUBLANES
VMEM_LIMIT = 48 * 1024 * 1024

TOKEN_TILE = 512
ATTN_TILE = 512
ATTN_CHAINS = 4
EXPERT_BLOCK = 512
ADALN_TILE = 512
ROUTE_ROWS = 8

_SEG_CQ = (0, 256)
_SEG_CKV = (256, 384)
_SEG_KR = (384, 512)
_SEG_U = (512, 1024)
_SEG_V = (1024, 1536)
_SEG_GMLA = (1536, 2560)
_SEG_GSGU = (2560, 3584)
PACKED_IN = 3584

Q_SCALE = (QK_DIM ** -0.5) * 1.4426950408889634

_NT_DIMS = (((1,), (1,)), ((), ()))


def _sigmoid(x):
    return 1.0 / (1.0 + jnp.exp(-x))


def _gelu_tanh(x):
    c = 0.7978845608028654
    return 0.5 * x * (1.0 + jnp.tanh(c * (x + 0.044715 * (x * x * x))))


def _rms(x, g):
    return x * lax.rsqrt(jnp.mean(x * x, axis=-1, keepdims=True) + EPS) * g


def _dot(a, b):
    return jnp.dot(a, b, preferred_element_type=F32)


def _dot_nt(a, b):
    return lax.dot_general(a, b, _NT_DIMS, preferred_element_type=F32)


def _adaln_kernel(ct_ref, w_ref, b_ref, o_ref):
    ct = ct_ref[...]
    s = ct * _sigmoid(ct)
    w = w_ref[...]
    rows = [jnp.sum(s[:, b:b + 1] * w, axis=0, keepdims=True) for b in range(ct.shape[1])]
    o_ref[...] = jnp.concatenate(rows, axis=0) + b_ref[...]


def _adaln(c, w_ada, b_ada):
    batch = c.shape[0]
    width = w_ada.shape[1]
    return pl.pallas_call(
        _adaln_kernel,
        grid=(width // ADALN_TILE,),
        in_specs=[
            pl.BlockSpec((D_MODEL, batch), lambda j: (0, 0)),
            pl.BlockSpec((D_MODEL, ADALN_TILE), lambda j: (0, j)),
            pl.BlockSpec((1, ADALN_TILE), lambda j: (0, j)),
        ],
        out_specs=pl.BlockSpec((batch, ADALN_TILE), lambda j: (0, j)),
        out_shape=jax.ShapeDtypeStruct((batch, width), F32),
        compiler_params=pltpu.CompilerParams(dimension_semantics=("arbitrary",)),
        name="adaln",
    )(c.T, w_ada, b_ada.reshape(1, width))


def _pre_kernel(x_ref, mod_ref, pos_ref, g1_ref, win_ref, qg_ref, wqa_ref, wqb_ref,
                kvg_ref, wk_ref, wv_ref, vones_ref, fcol_ref,
                vng_ref, vnb_ref, ws_ref, bs_ref, wbs_ref,
                qt_out, k_out, vt_out, gm_out, ms_out):
    tm = x_ref.shape[1]
    x = x_ref[0]
    shift1 = mod_ref[0, 0:1, :]
    scale1 = mod_ref[0, 1:2, :]
    h1 = _rms(x, g1_ref[...]) * (1.0 + scale1) + shift1
    hb = h1.astype(BF16)

    def proj(seg):
        return _dot(hb, win_ref[:, seg[0]:seg[1]])

    pos = pos_ref[0].astype(F32)
    ang = fcol_ref[...] * pos
    cos_t = jnp.cos(ang)
    sin_t = jnp.sin(ang)
    pad_rows = HEAD_PAD - QK_DIM

    cqn = _rms(proj(_SEG_CQ), qg_ref[...]).astype(BF16)
    qa_t = _dot_nt(wqa_ref[...], cqn)
    qb_t = _dot_nt(wqb_ref[...], cqn)
    c_tab = jnp.concatenate([jnp.ones((QK_NOPE_DIM, tm), F32), cos_t, cos_t,
                             jnp.ones((pad_rows, tm), F32)], axis=0) * Q_SCALE
    s_tab = jnp.concatenate([jnp.zeros((QK_NOPE_DIM, tm), F32), -sin_t, sin_t,
                             jnp.zeros((pad_rows, tm), F32)], axis=0) * Q_SCALE
    for h in range(MLA_HEADS):
        sl = slice(h * HEAD_PAD, (h + 1) * HEAD_PAD)
        qt_out[0, h, 0] = (qa_t[sl] * c_tab + qb_t[sl] * s_tab).astype(BF16)

    ckvn = _rms(proj(_SEG_CKV), kvg_ref[...]).astype(BF16)
    kr_t = proj(_SEG_KR).T
    x1 = kr_t[0:ROPE_HALF]
    x2 = kr_t[ROPE_HALF:QK_ROPE_DIM]
    k_pe = jnp.concatenate([jnp.zeros((QK_NOPE_DIM, tm), F32), x1 * cos_t - x2 * sin_t,
                            x1 * sin_t + x2 * cos_t, jnp.zeros((pad_rows, tm), F32)], axis=0).T
    k_full = _dot(ckvn, wk_ref[...])
    v_t = _dot_nt(wv_ref[...], ckvn) + vones_ref[...]
    for h in range(MLA_HEADS):
        sl = slice(h * HEAD_PAD, (h + 1) * HEAD_PAD)
        k_out[0, h] = (k_full[:, sl] + k_pe).astype(BF16)
        vt_out[0, h, 0] = v_t[h * V_ROWS:(h + 1) * V_ROWS].astype(BF16)

    gu = _gelu_tanh(proj(_SEG_U))
    gv = _gelu_tanh(proj(_SEG_V))
    mu = jnp.mean(gv, axis=-1, keepdims=True)
    dv = gv - mu
    var = jnp.mean(dv * dv, axis=-1, keepdims=True)
    vln = (dv * lax.rsqrt(var + EPS) * vng_ref[...] + vnb_ref[...]).astype(BF16)
    row = lax.broadcasted_iota(jnp.int32, (SGU_CHUNK, SGU_CHUNK), 0)
    col = lax.broadcasted_iota(jnp.int32, (SGU_CHUNK, SGU_CHUNK), 1)
    causal = col <= row
    w_tril = [jnp.where(causal, ws_ref[g], 0.0).astype(BF16) for g in range(SGU_GROUPS)]
    chunks = []
    for c in range(tm // SGU_CHUNK):
        vc = vln[c * SGU_CHUNK:(c + 1) * SGU_CHUNK]
        parts = [_dot(w_tril[g], vc[:, g * SGU_GROUP_DIM:(g + 1) * SGU_GROUP_DIM])
                 for g in range(SGU_GROUPS)]
        chunks.append(jnp.concatenate(parts, axis=1) + bs_ref[...])
    sgu = (gu * jnp.concatenate(chunks, axis=0)).astype(BF16)
    y_sgu = _dot(sgu, wbs_ref[...])
    ms_out[0] = (_sigmoid(proj(_SEG_GSGU)) * y_sgu).astype(BF16)
    gm_out[0] = _sigmoid(proj(_SEG_GMLA)).astype(BF16)


def _pre(x, mod, positions, g1, w_in_p, qg, wqa_t, wqb_t, kvg, wk, wv_t, vones, fcol,
         vng, vnb, w_s, bs_full, wbs):
    batch, seq, _ = x.shape
    tm = TOKEN_TILE
    assert tm == ATTN_TILE and seq % tm == 0
    nt = seq // tm
    hw = MLA_HEADS * HEAD_PAD

    def const(shape):
        return pl.BlockSpec(shape, lambda b, i: (0,) * len(shape))

    return pl.pallas_call(
        _pre_kernel,
        grid=(batch, nt),
        in_specs=[
            pl.BlockSpec((1, tm, D_MODEL), lambda b, i: (b, i, 0)),
            pl.BlockSpec((1, N_MOD, D_MODEL), lambda b, i: (b, 0, 0)),
            pl.BlockSpec((1, 1, tm), lambda b, i: (b, 0, i)),
            const((1, D_MODEL)),
            const((D_MODEL, PACKED_IN)),
            const((1, Q_LORA_RANK)),
            const((hw, Q_LORA_RANK)),
            const((hw, Q_LORA_RANK)),
            const((1, KV_LORA_RANK)),
            const((KV_LORA_RANK, hw)),
            const((MLA_HEADS * V_ROWS, KV_LORA_RANK)),
            const((MLA_HEADS * V_ROWS, 1)),
            const((ROPE_HALF, 1)),
            const((1, SGU_WIDTH)),
            const((1, SGU_WIDTH)),
            const((SGU_GROUPS, SGU_CHUNK, SGU_CHUNK)),
            const((SGU_CHUNK, SGU_WIDTH)),
            const((SGU_WIDTH, D_MODEL)),
        ],
        out_specs=[
            pl.BlockSpec((1, MLA_HEADS, 1, HEAD_PAD, tm), lambda b, i: (b, 0, i, 0, 0)),
            pl.BlockSpec((1, MLA_HEADS, tm, HEAD_PAD), lambda b, i: (b, 0, i, 0)),
            pl.BlockSpec((1, MLA_HEADS, 1, V_ROWS, tm), lambda b, i: (b, 0, i, 0, 0)),
            pl.BlockSpec((1, tm, D_MODEL), lambda b, i: (b, i, 0)),
            pl.BlockSpec((1, tm, D_MODEL), lambda b, i: (b, i, 0)),
        ],
        out_shape=[
            jax.ShapeDtypeStruct((batch, MLA_HEADS, nt, HEAD_PAD, tm), BF16),
            jax.ShapeDtypeStruct((batch, MLA_HEADS, seq, HEAD_PAD), BF16),
            jax.ShapeDtypeStruct((batch, MLA_HEADS, nt, V_ROWS, tm), BF16),
            jax.ShapeDtypeStruct((batch, seq, D_MODEL), BF16),
            jax.ShapeDtypeStruct((batch, seq, D_MODEL), BF16),
        ],
        compiler_params=pltpu.CompilerParams(
            dimension_semantics=("arbitrary", "arbitrary"), vmem_limit_bytes=VMEM_LIMIT),
        name="pre",
    )(x, mod, positions.reshape(batch, 1, seq), g1, w_in_p, qg, wqa_t, wqb_t, kvg, wk, wv_t, vones,
      fcol, vng, vnb, w_s, bs_full, wbs)


def _flash_kernel(qt_ref, k_ref, vt_ref, o_ref, m_ref, acc_ref, s0_ref, s1_ref):
    t = qt_ref.shape[4]
    i = pl.program_id(2)
    m_ref[...] = jnp.full(m_ref.shape, -1e30, F32)
    acc_ref[...] = jnp.zeros(acc_ref.shape, F32)

    def logits(c, j):
        start = pl.multiple_of(j * t, t)
        return _dot(k_ref[0, 0, pl.ds(start, t), :], qt_ref[0, 0, c])

    def update(c, j, s_ref, masked):
        def read():
            s = s_ref[c]
            if masked:
                key = lax.broadcasted_iota(jnp.int32, (t, t), 0)
                qry = lax.broadcasted_iota(jnp.int32, (t, t), 1)
                s = jnp.where(key <= qry, s, -jnp.inf)
            return s
        m_old = m_ref[c]
        m_new = jnp.maximum(m_old, jnp.max(read(), axis=0, keepdims=True))
        p = jnp.exp2(read() - m_new).astype(BF16)
        pv = _dot(vt_ref[0, 0, j], p)
        acc_ref[c] = jnp.exp2(m_old - m_new) * acc_ref[c] + pv
        m_ref[c] = m_new

    bufs = (s0_ref, s1_ref)
    chains = list(range(ATTN_CHAINS))
    last = ATTN_CHAINS - 1
    first = ATTN_CHAINS * i

    for c in chains:
        s0_ref[c] = logits(c, 0)
    for c in chains[:last]:
        s1_ref[c] = logits(c, 1)

    def body(u, carry):
        base = ATTN_CHAINS * u
        s1_ref[last] = logits(last, base + 1)
        for kk in range(ATTN_CHAINS):
            for c in chains:
                update(c, base + kk, bufs[kk % 2], False)
                if not (kk == ATTN_CHAINS - 1 and c == last):
                    bufs[kk % 2][c] = logits(c, base + kk + 2)
        return carry

    lax.fori_loop(0, i, body, 0)
    s1_ref[last] = logits(last, first + 1)
    for d in range(ATTN_CHAINS):
        for c in chains[d:]:
            update(c, first + d, bufs[d % 2], c == d)
            if c >= d + 2:
                bufs[d % 2][c] = logits(c, first + d + 2)
    for c in range(ATTN_CHAINS):
        acc = acc_ref[c]
        o_ref[0, 0, c] = (acc[0:V_HEAD_DIM] / acc[V_HEAD_DIM:V_HEAD_DIM + 1, :]).astype(BF16)


def _flash(qt, k, vt):
    batch, heads, nt, _, t = qt.shape
    seq = nt * t
    assert nt % ATTN_CHAINS == 0 and ATTN_CHAINS % 2 == 0
    return pl.pallas_call(
        _flash_kernel,
        grid=(batch, heads, nt // ATTN_CHAINS),
        in_specs=[
            pl.BlockSpec((1, 1, ATTN_CHAINS, HEAD_PAD, t), lambda b, h, i: (b, h, i, 0, 0)),
            pl.BlockSpec((1, 1, seq, HEAD_PAD), lambda b, h, i: (b, h, 0, 0)),
            pl.BlockSpec((1, 1, nt, V_ROWS, t), lambda b, h, i: (b, h, 0, 0, 0)),
        ],
        out_specs=pl.BlockSpec((1, 1, ATTN_CHAINS, V_HEAD_DIM, t), lambda b, h, i: (b, h, i, 0, 0)),
        out_shape=jax.ShapeDtypeStruct((batch, heads, nt, V_HEAD_DIM, t), BF16),
        scratch_shapes=[pltpu.VMEM((ATTN_CHAINS, 1, t), F32), pltpu.VMEM((ATTN_CHAINS, V_ROWS, t), F32),
                        pltpu.VMEM((ATTN_CHAINS, t, t), F32), pltpu.VMEM((ATTN_CHAINS, t, t), F32)],
        compiler_params=pltpu.CompilerParams(
            dimension_semantics=("arbitrary", "arbitrary", "arbitrary"),
            vmem_limit_bytes=VMEM_LIMIT),
        name="flash",
    )(qt, k, vt)


def _post_kernel(x_ref, o_ref, gm_ref, ms_ref, mod_ref, wbm_ref, wout_ref, g2_ref, rw_ref, rb_ref,
                 x2_out, h2_out, rt_out, rwt_out, cnt_out, carry_ref):
    tm = x_ref.shape[1]

    @pl.when((pl.program_id(0) == 0) & (pl.program_id(1) == 0))
    def _():
        carry_ref[...] = jnp.zeros(carry_ref.shape, F32)

    gate1 = mod_ref[0, 2:3, :]
    shift2 = mod_ref[0, 3:4, :]
    scale2 = mod_ref[0, 4:5, :]
    o_t = jnp.concatenate([o_ref[0, h, 0] for h in range(MLA_HEADS)], axis=0)
    y_mla = lax.dot_general(o_t, wbm_ref[...], (((0,), (0,)), ((), ())), preferred_element_type=F32)
    merged = gm_ref[0].astype(F32) * y_mla + ms_ref[0].astype(F32)
    x2 = x_ref[0] + gate1 * _dot(merged.astype(BF16), wout_ref[...])
    x2_out[0] = x2
    h2 = _rms(x2, g2_ref[...]) * (1.0 + scale2) + shift2
    h2_out[0] = h2.astype(BF16)

    hi = h2.astype(BF16)
    lo = (h2 - hi.astype(F32)).astype(BF16)
    a = _dot(hi, rw_ref[...])
    b = _dot(lo, rw_ref[:, 0:LANES])
    logits = a[:, 0:LANES] + a[:, LANES:2 * LANES] + b + rb_ref[...]

    lane = lax.broadcasted_iota(jnp.int32, (tm, LANES), 1).astype(F32)
    big = float(LANES)
    ninf = -jnp.inf
    gl = jnp.where(lane < N_GROUPS, logits, ninf)
    gmax = jnp.max(gl, axis=-1, keepdims=True)
    grp = jnp.min(jnp.where(gl == gmax, lane, big), axis=-1, keepdims=True)
    p_grp = 1.0 / jnp.sum(jnp.exp(gl - gmax), axis=-1, keepdims=True)
    first = N_GROUPS + EXPERTS_PER_GROUP * grp
    el = jnp.where((lane >= first) & (lane < first + EXPERTS_PER_GROUP), logits, ninf)
    m1 = jnp.max(el, axis=-1, keepdims=True)
    i1 = jnp.min(jnp.where(el == m1, lane, big), axis=-1, keepdims=True)
    el2 = jnp.where(lane == i1, ninf, el)
    m2 = jnp.max(el2, axis=-1, keepdims=True)
    i2 = jnp.min(jnp.where(el2 == m2, lane, big), axis=-1, keepdims=True)
    tt = jnp.exp(m2 - m1)
    w0 = p_grp / (1.0 + tt)
    w1 = p_grp * tt / (1.0 + tt)
    e0 = i1 - N_GROUPS
    e1 = i2 - N_GROUPS

    onehot = jnp.where((lane == e0) | (lane == e1), 1.0, 0.0)
    row = lax.broadcasted_iota(jnp.int32, (tm, tm), 0)
    col = lax.broadcasted_iota(jnp.int32, (tm, tm), 1)
    lower = jnp.where(col < row, 1.0, 0.0).astype(BF16)
    before = _dot(lower, onehot.astype(BF16)) + carry_ref[...]
    r0 = jnp.sum(jnp.where(lane == e0, before, 0.0), axis=-1, keepdims=True)
    r1 = jnp.sum(jnp.where(lane == e1, before, 0.0), axis=-1, keepdims=True)
    carry_ref[...] = carry_ref[...] + jnp.sum(onehot, axis=0, keepdims=True)
    cnt_out[...] = carry_ref[...]

    ri = jnp.where(lane == 0, e0, jnp.where(lane == 1, e1, jnp.where(lane == 2, r0, jnp.where(lane == 3, r1, 0.0))))
    rt_out[0, 0] = ri.T[0:ROUTE_ROWS]
    rwt_out[0] = jnp.where(lane == 0, w0, jnp.where(lane == 1, w1, 0.0))


def _post(x, o, gm, ms, mod, wbm, wout, g2, rw, rb):
    batch, seq, _ = x.shape
    tm = TOKEN_TILE

    def const(shape):
        return pl.BlockSpec(shape, lambda b, i: (0,) * len(shape))

    tok = lambda w: pl.BlockSpec((1, tm, w), lambda b, i: (b, i, 0))
    return pl.pallas_call(
        _post_kernel,
        grid=(batch, seq // tm),
        in_specs=[
            tok(D_MODEL),
            pl.BlockSpec((1, MLA_HEADS, 1, V_HEAD_DIM, tm), lambda b, i: (b, 0, i, 0, 0)),
            tok(D_MODEL),
            tok(D_MODEL),
            pl.BlockSpec((1, N_MOD, D_MODEL), lambda b, i: (b, 0, 0)),
            const((MLA_HEADS * V_HEAD_DIM, D_MODEL)),
            const((D_MODEL, D_MODEL)),
            const((1, D_MODEL)),
            const((D_MODEL, 2 * LANES)),
            const((1, LANES)),
        ],
        out_specs=[tok(D_MODEL), tok(D_MODEL),
                   pl.BlockSpec((1, 1, ROUTE_ROWS, tm), lambda b, i: (b, i, 0, 0)),
                   tok(LANES), const((1, LANES))],
        out_shape=[
            jax.ShapeDtypeStruct((batch, seq, D_MODEL), F32),
            jax.ShapeDtypeStruct((batch, seq, D_MODEL), BF16),
            jax.ShapeDtypeStruct((batch, seq // tm, ROUTE_ROWS, tm), F32),
            jax.ShapeDtypeStruct((batch, seq, LANES), F32),
            jax.ShapeDtypeStruct((1, LANES), F32),
        ],
        scratch_shapes=[pltpu.VMEM((1, LANES), F32)],
        compiler_params=pltpu.CompilerParams(
            dimension_semantics=("arbitrary", "arbitrary"), vmem_limit_bytes=VMEM_LIMIT),
        name="post",
    )(x, o, gm, ms, mod, wbm, wout, g2, rw, rb)


def _expert_kernel(be_ref, nr_ref, x_ref, w1_ref, w3_ref, w2_ref, y_ref):
    j = pl.program_id(0)

    @pl.when(j < nr_ref[0])
    def _():
        x = x_ref[...]
        a = _dot(x, w1_ref[0, 0].astype(BF16))
        b = _dot(x, w3_ref[0, 0].astype(BF16))
        act = (a * _sigmoid(a) * b).astype(BF16)
        y_ref[...] = _dot(act, w2_ref[0, 0].astype(BF16)).astype(BF16)

    @pl.when(j >= nr_ref[0])
    def _():
        y_ref[...] = jnp.zeros(y_ref.shape, BF16)


def _experts(blk_e, n_real, x_sorted, w1, w3, w2):
    rows = x_sorted.shape[0]
    nblk = rows // EXPERT_BLOCK
    last = lambda j, be, nr: jnp.minimum(j, nr[0] - 1)
    w_map = lambda j, be, nr: (0, be[last(j, be, nr)], 0, 0)
    return pl.pallas_call(
        _expert_kernel,
        grid_spec=pltpu.PrefetchScalarGridSpec(
            num_scalar_prefetch=2,
            grid=(nblk,),
            in_specs=[
                pl.BlockSpec((EXPERT_BLOCK, D_MODEL), lambda j, be, nr: (last(j, be, nr), 0)),
                pl.BlockSpec((1, 1, D_MODEL, EXPERT_FF), w_map),
                pl.BlockSpec((1, 1, D_MODEL, EXPERT_FF), w_map),
                pl.BlockSpec((1, 1, EXPERT_FF, D_MODEL), w_map),
            ],
            out_specs=pl.BlockSpec((EXPERT_BLOCK, D_MODEL), lambda j, be, nr: (j, 0)),
        ),
        out_shape=jax.ShapeDtypeStruct((rows, D_MODEL), BF16),
        compiler_params=pltpu.CompilerParams(
            dimension_semantics=("arbitrary",), vmem_limit_bytes=VMEM_LIMIT),
        name="experts",
    )(blk_e, n_real, x_sorted, w1, w3, w2)


def _final_kernel(x2_ref, y0_ref, y1_ref, rwt_ref, mod_ref, fg_ref, out_ref):
    gate2 = mod_ref[0, 5:6, :]
    rwt = rwt_ref[0]
    moe = rwt[:, 0:1] * y0_ref[0].astype(F32) + rwt[:, 1:2] * y1_ref[0].astype(F32)
    x3 = x2_ref[0] + gate2 * moe
    out_ref[0] = _rms(x3, fg_ref[...])


def _final(x2, y0, y1, rwt, mod, fg):
    batch, seq, _ = x2.shape
    tm = TOKEN_TILE
    tok = lambda w: pl.BlockSpec((1, tm, w), lambda b, i: (b, i, 0))
    return pl.pallas_call(
        _final_kernel,
        grid=(batch, seq // tm),
        in_specs=[
            tok(D_MODEL), tok(D_MODEL), tok(D_MODEL), tok(LANES),
            pl.BlockSpec((1, N_MOD, D_MODEL), lambda b, i: (b, 0, 0)),
            pl.BlockSpec((1, D_MODEL), lambda b, i: (0, 0)),
        ],
        out_specs=tok(D_MODEL),
        out_shape=jax.ShapeDtypeStruct((batch, seq, D_MODEL), F32),
        compiler_params=pltpu.CompilerParams(
            dimension_semantics=("arbitrary", "arbitrary"), vmem_limit_bytes=VMEM_LIMIT),
        name="final",
    )(x2, y0, y1, rwt, mod, fg)


def _pack_weights(w_in, w_uq, w_ukv, w_br_mla, w_rg, b_rg, w_re, b_re, b_s):
    kr_end = Q_LORA_RANK + KV_LORA_RANK + QK_ROPE_DIM
    w_in_p = jnp.concatenate(
        [w_in[:, :kr_end], jnp.zeros((D_MODEL, _SEG_KR[1] - kr_end), F32), w_in[:, kr_end:]],
        axis=1).astype(BF16)
    pad = HEAD_PAD - QK_DIM
    hw = MLA_HEADS * HEAD_PAD
    wq = w_uq.reshape(Q_LORA_RANK, MLA_HEADS, QK_DIM)
    nope, r1, r2 = wq[..., :QK_NOPE_DIM], wq[..., QK_NOPE_DIM:QK_NOPE_DIM + ROPE_HALF], wq[..., QK_NOPE_DIM + ROPE_HALF:]
    zpad = jnp.zeros((Q_LORA_RANK, MLA_HEADS, pad), F32)
    wqa_t = jnp.concatenate([nope, r1, r2, zpad], axis=-1).reshape(Q_LORA_RANK, hw).T.astype(BF16)
    wqb_t = jnp.concatenate([jnp.zeros_like(nope), r2, r1, zpad], axis=-1).reshape(Q_LORA_RANK, hw).T.astype(BF16)
    wkv = w_ukv.reshape(KV_LORA_RANK, MLA_HEADS, QK_NOPE_DIM + V_HEAD_DIM)
    wk = jnp.concatenate([wkv[..., :QK_NOPE_DIM], jnp.zeros((KV_LORA_RANK, MLA_HEADS, HEAD_PAD - QK_NOPE_DIM), F32)],
                         axis=-1).reshape(KV_LORA_RANK, hw).astype(BF16)
    wv_t = jnp.concatenate([wkv[..., QK_NOPE_DIM:], jnp.zeros((KV_LORA_RANK, MLA_HEADS, V_ROWS - V_HEAD_DIM), F32)],
                           axis=-1).reshape(KV_LORA_RANK, MLA_HEADS * V_ROWS).T.astype(BF16)
    vones = jnp.zeros((MLA_HEADS, V_ROWS), F32).at[:, V_HEAD_DIM].set(1.0).reshape(MLA_HEADS * V_ROWS, 1)
    wbm = w_br_mla.astype(BF16)
    n_r = N_GROUPS + N_EXPERTS
    wr = jnp.concatenate([w_rg, w_re, jnp.zeros((D_MODEL, LANES - n_r), F32)], axis=1)
    wr_hi = wr.astype(BF16)
    wr_lo = (wr - wr_hi.astype(F32)).astype(BF16)
    rw = jnp.concatenate([wr_hi, wr_lo], axis=1)
    rb = jnp.concatenate([b_rg, b_re, jnp.zeros((LANES - n_r,), F32)]).reshape(1, LANES)
    bs_full = jnp.repeat(b_s.T, SGU_GROUP_DIM, axis=1)
    fcol = (ROPE_THETA ** (-jnp.arange(ROPE_HALF, dtype=F32) / ROPE_HALF)).reshape(ROPE_HALF, 1)
    return w_in_p, wqa_t, wqb_t, wk, wv_t, vones, wbm, rw, rb, bs_full, fcol


def kernel(x, c, positions, w_ada, b_ada, norm1_g, w_in, q_norm_g, w_uq, kv_norm_g, w_ukv, v_norm_g, v_norm_b, w_s, b_s, w_br_mla, w_br_sgu, w_out, norm2_g, w_rg, b_rg, w_re, b_re, w1, w3, w2, final_g):
    batch, seq, _ = x.shape
    n_tok = batch * seq
    assert w_ada.shape[0] == 1, "single-layer block"
    l = 0
    row = lambda v: v.reshape(1, -1)
    (w_in_p, wqa_t, wqb_t, wk, wv_t, vones, wbm, rw, rb, bs_full, fcol) = _pack_weights(
        w_in[l], w_uq[l], w_ukv[l], w_br_mla[l], w_rg[l], b_rg[l], w_re[l], b_re[l], b_s[l])
    mod = _adaln(c, w_ada[l], b_ada[l]).reshape(batch, N_MOD, D_MODEL)
    qt, k, vt, gm, ms = _pre(
        x, mod, positions, row(norm1_g[l]), w_in_p, row(q_norm_g[l]), wqa_t, wqb_t, row(kv_norm_g[l]),
        wk, wv_t, vones, fcol, row(v_norm_g[l]), row(v_norm_b[l]), w_s[l], bs_full,
        w_br_sgu[l].astype(BF16))
    o = _flash(qt, k, vt)
    x2, h2, rt, rwt, cnt = _post(x, o, gm, ms, mod, wbm, w_out[l].astype(BF16), row(norm2_g[l]), rw, rb)

    counts = cnt[0, :N_EXPERTS].astype(jnp.int32)
    padded = ((counts + EXPERT_BLOCK - 1) // EXPERT_BLOCK) * EXPERT_BLOCK
    pad_ends = jnp.cumsum(padded)
    pad_starts = pad_ends - padded
    n_assign = n_tok * TOP_K
    rows_total = ((n_assign + EXPERT_BLOCK - 1) // EXPERT_BLOCK + N_EXPERTS) * EXPERT_BLOCK
    nblk = rows_total // EXPERT_BLOCK
    rti = rt.astype(jnp.int32)
    expert_ids = jnp.arange(N_EXPERTS, dtype=jnp.int32)
    dests = []
    for kk in range(TOP_K):
        e_k = rti[:, :, kk, :].reshape(n_tok)
        r_k = rti[:, :, TOP_K + kk, :].reshape(n_tok)
        start_k = jnp.sum(jnp.where(e_k[:, None] == expert_ids[None, :], pad_starts[None, :], 0), axis=1)
        dests.append(start_k + r_k)
    tok_ids = jnp.arange(n_tok, dtype=jnp.int32)
    slot_tok = (jnp.arange(rows_total, dtype=jnp.int32) % n_tok).at[jnp.concatenate(dests)].set(
        jnp.concatenate([tok_ids] * TOP_K), unique_indices=True, mode='promise_in_bounds')
    blk_start = jnp.arange(nblk, dtype=jnp.int32) * EXPERT_BLOCK
    blk_e = jnp.minimum(jnp.sum((pad_ends[None, :] <= blk_start[:, None]).astype(jnp.int32), axis=1),
                        N_EXPERTS - 1)
    n_real = pad_ends[-1:] // EXPERT_BLOCK
    x_sorted = h2.reshape(n_tok, D_MODEL)[slot_tok]
    y = _experts(blk_e, n_real, x_sorted, w1, w3, w2)
    y0, y1 = [y[d].reshape(batch, seq, D_MODEL) for d in dests]
    return _final(x2, y0, y1, rwt, mod, row(final_g))
```
